```python
import math
import jax, jax.numpy as jnp
from jax import lax
import numpy as np

D_MODEL = 1024
BATCH = 32
SEQ = 2048
DEPTH = 1

N_HEADS = 16
HEAD_DIM = 64
ATTN_W = N_HEADS * HEAD_DIM
Q_BLOCK = 128
LRU_W = D_MODEL
N_LRU_BLOCKS = 16
LRU_BLOCK = LRU_W // N_LRU_BLOCKS
CONV_W = 4
LRU_C = 8.0
N_BRANCHES = 2
IN_COLS = 3 * ATTN_W + N_HEADS + 2 * LRU_W + N_BRANCHES * D_MODEL
SPLITS = (ATTN_W, 2 * ATTN_W, 3 * ATTN_W, 3 * ATTN_W + N_HEADS,
          3 * ATTN_W + N_HEADS + LRU_W, 3 * ATTN_W + N_HEADS + 2 * LRU_W)
N_EXPERTS = 32
TOP_K = 4
D_EXPERT = D_MODEL
SWIGLU_ALPHA = 1.702
SWIGLU_LIMIT = 7.0
EXPERT_BLOCK = 256
RMS_EPS = 1e-6

kernel_name = "hybrid_fox_rglru_moe_block"


def rmsnorm(x, scale):
    xf = x.astype(jnp.float32)
    y = xf * lax.rsqrt(jnp.mean(xf * xf, axis=-1, keepdims=True) + RMS_EPS)
    return (y * scale.astype(jnp.float32)).astype(x.dtype)


def forgetting_attention(q, k, v, log_f):
    S = q.shape[1]
    c = jnp.cumsum(log_f, axis=1).transpose(0, 2, 1)
    scale = HEAD_DIM ** -0.5
    outs = []
    for i in range(S // Q_BLOCK):
        q0, q1 = i * Q_BLOCK, (i + 1) * Q_BLOCK
        s = jnp.einsum('bqhd,bkhd->bhqk', q[:, q0:q1], k[:, :q1]).astype(jnp.float32) * scale
        bias = c[:, :, q0:q1, None] - c[:, :, None, :q1]
        causal = (q0 + jnp.arange(Q_BLOCK))[:, None] >= jnp.arange(q1)[None, :]
        s = jnp.where(causal, s + bias, -jnp.inf)
        p = jax.nn.softmax(s, axis=-1).astype(v.dtype)
        outs.append(jnp.einsum('bhqk,bkhd->bqhd', p, v[:, :q1]))
    return jnp.concatenate(outs, axis=1)


def causal_dwconv(x, w, b):
    S = x.shape[1]
    xp = jnp.pad(x, ((0, 0), (CONV_W - 1, 0), (0, 0)))
    return sum(xp[:, j:j + S] * w[j] for j in range(CONV_W)) + b


def block_diag_linear(xh, w, b):
    return jnp.einsum('bsni,nij->bsnj', xh, w) + b


def rg_lru(x, wa, ba, wx, bx, lam):
    B, S, W = x.shape
    xh = x.reshape(B, S, N_LRU_BLOCKS, LRU_BLOCK)
    r = jax.nn.sigmoid(block_diag_linear(xh, wa, ba)).reshape(B, S, W)
    i = jax.nn.sigmoid(block_diag_linear(xh, wx, bx)).reshape(B, S, W)
    log_a = -LRU_C * r.astype(jnp.float32) * jax.nn.softplus(-lam.astype(jnp.float32))
    a = jnp.exp(log_a)
    mult = jnp.sqrt(-jnp.expm1(2.0 * log_a))
    mult = jnp.where(jnp.arange(S)[None, :, None] == 0, 1.0, mult)
    b = mult * (i * x).astype(jnp.float32)

    def combine(left, right):
        a1, b1 = left
        a2, b2 = right
        return a1 * a2, a2 * b1 + b2

    _, h = lax.associative_scan(combine, (a, b), axis=1)
    return h.astype(x.dtype)


def hybrid_mixer(xn, w_in, f_bias, gate_bias, conv_w, conv_b, rg_wa, rg_ba, rg_wx, rg_bx,
                 rg_lambda, w_attn_proj, w_rec_proj, w_out):
    B, S, _ = xn.shape
    proj = xn @ w_in
    q, k, v, f_logit, x_rec, y_rec, g_logits = jnp.split(proj, SPLITS, axis=-1)
    q = q.reshape(B, S, N_HEADS, HEAD_DIM)
    k = k.reshape(B, S, N_HEADS, HEAD_DIM)
    v = v.reshape(B, S, N_HEADS, HEAD_DIM)
    log_f = jax.nn.log_sigmoid((f_logit + f_bias).astype(jnp.float32))
    attn = forgetting_attention(q, k, v, log_f).reshape(B, S, ATTN_W)
    rec = rg_lru(causal_dwconv(x_rec, conv_w, conv_b), rg_wa, rg_ba, rg_wx, rg_bx, rg_lambda)
    rec = rec * jax.nn.gelu(y_rec)
    g_attn, g_rec = jnp.split(jax.nn.sigmoid(g_logits + gate_bias), N_BRANCHES, axis=-1)
    merged = g_attn * (attn @ w_attn_proj) + g_rec * (rec @ w_rec_proj)
    return merged @ w_out


def clamped_swiglu(h):
    glu, lin = jnp.split(h, 2, axis=-1)
    glu = jnp.minimum(glu, SWIGLU_LIMIT)
    lin = jnp.clip(lin, -SWIGLU_LIMIT, SWIGLU_LIMIT)
    return glu * jax.nn.sigmoid(SWIGLU_ALPHA * glu) * (lin + 1.0)


def moe_ffn(xn, router_w, router_b, w1, b1, w2, b2):
    B, S, D = xn.shape
    T = B * S
    A = T * TOP_K
    xt = xn.reshape(T, D)
    logits = (xt @ router_w + router_b).astype(jnp.float32)
    top_vals, top_idx = lax.top_k(logits, TOP_K)
    gates = jax.nn.softmax(top_vals, axis=-1)
    flat_e = top_idx.reshape(-1)
    flat_tok = jnp.arange(A, dtype=jnp.int32) // TOP_K
    flat_gate = gates.reshape(-1)
    order = jnp.argsort(flat_e)
    se, stok, sgate = flat_e[order], flat_tok[order], flat_gate[order]
    counts = jnp.bincount(flat_e, length=N_EXPERTS)
    padded = (counts + EXPERT_BLOCK - 1) // EXPERT_BLOCK * EXPERT_BLOCK
    pad_end = jnp.cumsum(padded)
    pad_start = pad_end - padded
    start = jnp.cumsum(counts) - counts
    dest = pad_start[se] + jnp.arange(A) - start[se]
    n_blocks = (A + EXPERT_BLOCK - 1) // EXPERT_BLOCK + N_EXPERTS
    P = n_blocks * EXPERT_BLOCK
    row_tok = jnp.full((P,), T, dtype=jnp.int32).at[dest].set(stok)
    row_gate = jnp.zeros((P,), jnp.float32).at[dest].set(sgate)
    block_expert = jnp.minimum(
        jnp.searchsorted(pad_end, jnp.arange(n_blocks) * EXPERT_BLOCK, side='right'), N_EXPERTS - 1)
    x_pad = jnp.concatenate([xt, jnp.zeros((1, D), xt.dtype)], axis=0)

    def step(acc, blk):
        e, tok, g = blk
        h = x_pad[tok] @ w1[e] + b1[e]
        y = clamped_swiglu(h) @ w2[e] + b2[e]
        return acc.at[tok].add(y * g[:, None].astype(y.dtype)), None

    acc, _ = lax.scan(step, jnp.zeros((T + 1, D), xt.dtype),
                      (block_expert, row_tok.reshape(n_blocks, EXPERT_BLOCK),
                       row_gate.reshape(n_blocks, EXPERT_BLOCK)))
    return acc[:T].reshape(B, S, D)


def setup_inputs(seed: int = 0) -> dict:
    key = jax.random.key(seed)
    ks = jax.random.split(key, 24)
    L, D, E, I = DEPTH, D_MODEL, N_EXPERTS, D_EXPERT
    nrm = lambda k, shape, s: jax.random.normal(k, shape, jnp.float32) * s
    u = jax.random.uniform(ks[9], (L, LRU_W), jnp.float32, 0.9, 0.999)
    p = u ** (1.0 / LRU_C)
    return {
        "x": nrm(ks[0], (BATCH, SEQ, D), 1.0),
        "norm1_scale": 1.0 + nrm(ks[1], (L, D), 0.02),
        "w_in": nrm(ks[2], (L, D, IN_COLS), D ** -0.5),
        "f_bias": 2.0 + nrm(ks[3], (L, N_HEADS), 0.5),
        "gate_bias": nrm(ks[4], (L, N_BRANCHES * D), 0.02),
        "conv_w": nrm(ks[5], (L, CONV_W, LRU_W), CONV_W ** -0.5),
        "conv_b": nrm(ks[6], (L, LRU_W), 0.02),
        "rg_wa": nrm(ks[7], (L, N_LRU_BLOCKS, LRU_BLOCK, LRU_BLOCK), LRU_BLOCK ** -0.5),
        "rg_ba": nrm(ks[8], (L, N_LRU_BLOCKS, LRU_BLOCK), 0.02),
        "rg_wx": nrm(ks[10], (L, N_LRU_BLOCKS, LRU_BLOCK, LRU_BLOCK), LRU_BLOCK ** -0.5),
        "rg_bx": nrm(ks[11], (L, N_LRU_BLOCKS, LRU_BLOCK), 0.02),
        "rg_lambda": jnp.log(p / (1.0 - p)),
        "w_attn_proj": nrm(ks[12], (L, ATTN_W, D), ATTN_W ** -0.5),
        "w_rec_proj": nrm(ks[13], (L, LRU_W, D), LRU_W ** -0.5),
        "w_out": nrm(ks[14], (L, D, D), D ** -0.5),
        "norm2_scale": 1.0 + nrm(ks[15], (L, D), 0.02),
        "router_w": nrm(ks[16], (L, D, E), D ** -0.5),
        "router_b": nrm(ks[17], (L, E), 0.01),
        "moe_w1": nrm(ks[18], (L, E, D, 2 * I), D ** -0.5),
        "moe_b1": nrm(ks[19], (L, E, 2 * I), 0.02),
        "moe_w2": nrm(ks[20], (L, E, I, D), I ** -0.5),
        "moe_b2": nrm(ks[21], (L, E, D), 0.02),
        "final_norm_scale": 1.0 + nrm(ks[22], (D,), 0.02),
    }


def reference(x, norm1_scale, w_in, f_bias, gate_bias, conv_w, conv_b, rg_wa, rg_ba, rg_wx,
              rg_bx, rg_lambda, w_attn_proj, w_rec_proj, w_out, norm2_scale, router_w, router_b,
              moe_w1, moe_b1, moe_w2, moe_b2, final_norm_scale):
    for l in range(DEPTH):
        xn = rmsnorm(x, norm1_scale[l])
        x = x + hybrid_mixer(xn, w_in[l], f_bias[l], gate_bias[l], conv_w[l], conv_b[l],
                             rg_wa[l], rg_ba[l], rg_wx[l], rg_bx[l], rg_lambda[l],
                             w_attn_proj[l], w_rec_proj[l], w_out[l])
        xn = rmsnorm(x, norm2_scale[l])
        x = x + moe_ffn(xn, router_w[l], router_b[l], moe_w1[l], moe_b1[l], moe_w2[l], moe_b2[l])
    return rmsnorm(x, final_norm_scale)
```

```python
import functools

import jax
import jax.numpy as jnp
from jax import lax
from jax.experimental import pallas as pl
from jax.experimental.pallas import tpu as pltpu

D_MODEL = 1024
N_HEADS = 16
HEAD_DIM = 64
LANES = 128
HEADS_PER_SLAB = LANES // HEAD_DIM
N_SLABS = D_MODEL // LANES
N_LRU_BLOCKS = 16
LRU_BLOCK = 64
LRU_GROUP = 256
N_LRU_GROUPS = D_MODEL // LRU_GROUP
CONV_W = 4
LRU_C = 8.0
N_EXPERTS = 32
TOP_K = 4
SWIGLU_ALPHA = 1.702
SWIGLU_LIMIT = 7.0
RMS_EPS = 1e-6
NEG_BIG = -1e30

PROJ_COLS = 7 * D_MODEL
COL_Q, COL_K, COL_V, COL_XREC, COL_YREC, COL_GA, COL_GR = range(7)

VMEM_LIMIT = 56 * 1024 * 1024


def _cparams(sem):
    return pltpu.CompilerParams(dimension_semantics=sem, vmem_limit_bytes=VMEM_LIMIT)


def _softplus(z):
    return jnp.maximum(z, 0.0) + jnp.log1p(jnp.exp(-jnp.abs(z)))


def _rms(x, scale):
    return x * lax.rsqrt(jnp.mean(x * x, axis=-1, keepdims=True) + RMS_EPS) * scale


def _in_proj_kernel(x_ref, scale_ref, w_ref, wf_ref, proj_ref, f_ref, xn_ref):
    @pl.when(pl.program_id(1) == 0)
    def _():
        xn = _rms(x_ref[...], scale_ref[...]).astype(jnp.bfloat16)
        xn_ref[...] = xn
        f_ref[...] = jnp.dot(xn, wf_ref[...], preferred_element_type=jnp.float32)

    proj_ref[...] = jnp.dot(xn_ref[...], w_ref[...],
                            preferred_element_type=jnp.float32).astype(proj_ref.dtype)


def _in_proj(x2, scale, w, wf, tm):
    T = x2.shape[0]
    n_col = w.shape[1] // D_MODEL
    return pl.pallas_call(
        _in_proj_kernel,
        grid=(T // tm, n_col),
        in_specs=[
            pl.BlockSpec((tm, D_MODEL), lambda i, j: (i, 0)),
            pl.BlockSpec((1, D_MODEL), lambda i, j: (0, 0)),
            pl.BlockSpec((D_MODEL, D_MODEL), lambda i, j: (0, j)),
            pl.BlockSpec((D_MODEL, LANES), lambda i, j: (0, 0)),
        ],
        out_specs=[
            pl.BlockSpec((tm, D_MODEL), lambda i, j: (i, j)),
            pl.BlockSpec((tm, LANES), lambda i, j: (i, 0)),
        ],
        out_shape=[
            jax.ShapeDtypeStruct((T, w.shape[1]), jnp.bfloat16),
            jax.ShapeDtypeStruct((T, LANES), jnp.float32),
        ],
        scratch_shapes=[pltpu.VMEM((tm, D_MODEL), jnp.bfloat16)],
        compiler_params=_cparams(("parallel", "arbitrary")),
        name="in_proj",
    )(x2, scale, w, wf)


def _fgate_kernel(f_ref, fb_ref, c_ref, *, chunk):
    S = f_ref.shape[0]
    row = lax.broadcasted_iota(jnp.int32, (chunk, chunk), 0)
    col = lax.broadcasted_iota(jnp.int32, (chunk, chunk), 1)
    tri = (row >= col).astype(jnp.float32)
    carry = jnp.zeros((1, LANES), jnp.float32)
    for c in range(S // chunk):
        z = f_ref[pl.ds(c * chunk, chunk), :] + fb_ref[...]
        log_f = -_softplus(-z)
        cs = jnp.dot(tri, log_f, precision=lax.Precision.HIGHEST,
                     preferred_element_type=jnp.float32) + carry
        carry = cs[chunk - 1:chunk, :]
        c_ref[:, pl.ds(c * chunk, chunk)] = cs.T[:N_HEADS, :]


def _fgate(f3, fb, chunk=256):
    B, S, _ = f3.shape
    return pl.pallas_call(
        functools.partial(_fgate_kernel, chunk=chunk),
        grid=(B,),
        in_specs=[
            pl.BlockSpec((None, S, LANES), lambda b: (b, 0, 0)),
            pl.BlockSpec((1, LANES), lambda b: (0, 0)),
        ],
        out_specs=pl.BlockSpec((None, N_HEADS, S), lambda b: (b, 0, 0)),
        out_shape=jax.ShapeDtypeStruct((B, N_HEADS, S), jnp.float32),
        compiler_params=_cparams(("parallel",)),
        name="fgate",
    )(f3, fb)


def _attn_kernel(q_ref, k_ref, v_ref, c_ref, o_ref, *, tq):
    qi = pl.program_id(2)
    q = q_ref[...]
    lane = lax.broadcasted_iota(jnp.int32, (1, LANES), 1)
    row = lax.broadcasted_iota(jnp.int32, (tq, tq), 0)
    col = lax.broadcasted_iota(jnp.int32, (tq, tq), 1)
    causal = row >= col
    outs = []
    for h in range(HEADS_PER_SLAB):
        in_head = (lane // HEAD_DIM) == h
        qh = jnp.where(in_head, q, jnp.zeros_like(q))

        def scores(kc, qh=qh, h=h):
            start = pl.multiple_of(kc * tq, tq)
            ks = k_ref[pl.ds(start, tq), :]
            s = lax.dot_general(qh, ks, (((1,), (1,)), ((), ())),
                                preferred_element_type=jnp.float32)
            return s - c_ref[h:h + 1, pl.ds(start, tq)], start

        def update(s, start, carry):
            m, l, acc = carry
            m_new = jnp.maximum(m, jnp.max(s, axis=-1, keepdims=True))
            alpha = jnp.exp(m - m_new)
            p = jnp.exp(s - m_new)
            l = alpha * l + jnp.sum(p, axis=-1, keepdims=True)
            pv = jnp.dot(p.astype(jnp.bfloat16), v_ref[pl.ds(start, tq), :],
                         preferred_element_type=jnp.float32)
            return m_new, l, alpha * acc + pv

        def body(kc, carry):
            s, start = scores(kc)
            return update(s, start, carry)

        init = (jnp.full((tq, 1), NEG_BIG, jnp.float32),
                jnp.zeros((tq, 1), jnp.float32),
                jnp.zeros((tq, LANES), jnp.float32))
        carry = lax.fori_loop(0, qi, body, init)
        s, start = scores(qi)
        _, l, acc = update(jnp.where(causal, s, NEG_BIG), start, carry)
        outs.append(acc / l)
    o_ref[...] = jnp.where(lane < HEAD_DIM, outs[0], outs[1]).astype(o_ref.dtype)


def _attention(proj3, c4, tq):
    B, S, _ = proj3.shape
    return pl.pallas_call(
        functools.partial(_attn_kernel, tq=tq),
        grid=(B, N_SLABS, S // tq),
        in_specs=[
            pl.BlockSpec((None, tq, LANES), lambda b, p, i: (b, i, COL_Q * N_SLABS + p)),
            pl.BlockSpec((None, S, LANES), lambda b, p, i: (b, 0, COL_K * N_SLABS + p)),
            pl.BlockSpec((None, S, LANES), lambda b, p, i: (b, 0, COL_V * N_SLABS + p)),
            pl.BlockSpec((None, None, HEADS_PER_SLAB, S), lambda b, p, i: (b, p, 0, 0)),
        ],
        out_specs=pl.BlockSpec((None, tq, LANES), lambda b, p, i: (b, i, p)),
        out_shape=jax.ShapeDtypeStruct((B, S, D_MODEL), jnp.bfloat16),
        compiler_params=_cparams(("parallel", "parallel", "arbitrary")),
        name="attn",
    )(proj3, proj3, proj3, c4)


def _rglru_kernel(x_ref, y_ref, cw_ref, cb_ref, wg_ref, ba_ref, bx_ref, lam_ref, o_ref,
                  xe_ref, a_ref, b_ref, h_ref, *, ts):
    HALO = 8
    s_idx = pl.program_id(1)

    @pl.when(s_idx == 0)
    def _():
        xe_ref[pl.ds(0, HALO), :] = jnp.zeros((HALO, D_MODEL), jnp.float32)
        h_ref[...] = jnp.zeros_like(h_ref)

    xe_ref[pl.ds(HALO, ts), :] = x_ref[...].astype(jnp.float32)
    conv = cb_ref[...] + cw_ref[CONV_W - 1:CONV_W, :] * xe_ref[pl.ds(HALO, ts), :]
    for j in range(CONV_W - 1):
        shift = CONV_W - 1 - j
        conv = conv + cw_ref[j:j + 1, :] * xe_ref[pl.ds(HALO - shift, ts), :]
    xe_ref[pl.ds(0, HALO), :] = xe_ref[pl.ds(ts, HALO), :]

    cb16 = conv.astype(jnp.bfloat16)
    sp = _softplus(-lam_ref[...])
    first = (lax.broadcasted_iota(jnp.int32, (ts, 1), 0) + s_idx * ts) == 0
    for g in range(N_LRU_GROUPS):
        cols = slice(g * LRU_GROUP, (g + 1) * LRU_GROUP)
        gates = jnp.dot(cb16[:, cols], wg_ref[g], preferred_element_type=jnp.float32)
        r = jax.nn.sigmoid(gates[:, :LRU_GROUP] + ba_ref[:, cols])
        i = jax.nn.sigmoid(gates[:, LRU_GROUP:] + bx_ref[:, cols])
        log_a = -LRU_C * r * sp[:, cols]
        a = jnp.exp(log_a)
        mult = jnp.where(first, 1.0, jnp.sqrt(1.0 - a * a))
        a_ref[:, cols] = a
        b_ref[:, cols] = mult * (i * conv[:, cols])

    SUB = 8
    rows = lax.broadcasted_iota(jnp.int32, (SUB, D_MODEL), 0)

    def slab(i, h_prev):
        start = pl.multiple_of(i * SUB, SUB)
        a = a_ref[pl.ds(start, SUB), :]
        b = b_ref[pl.ds(start, SUB), :]
        for d in (1, 2, 4):
            keep = rows >= d
            a_s = jnp.where(keep, pltpu.roll(a, d, 0), 1.0)
            b_s = jnp.where(keep, pltpu.roll(b, d, 0), 0.0)
            b = a * b_s + b
            a = a * a_s
        h = a * h_prev + b
        gate = jax.nn.gelu(y_ref[pl.ds(start, SUB), :].astype(jnp.float32))
        o_ref[pl.ds(start, SUB), :] = (h * gate).astype(o_ref.dtype)
        return jnp.broadcast_to(h[SUB - 1:SUB, :], (SUB, D_MODEL))

    h_ref[...] = lax.fori_loop(0, ts // SUB, slab, h_ref[...], unroll=4)


def _rglru(proj3, conv_w, conv_b, wg, ba, bx, lam, ts):
    B, S, _ = proj3.shape
    full = lambda shape: pl.BlockSpec(shape, lambda b, s: (0,) * len(shape))
    return pl.pallas_call(
        functools.partial(_rglru_kernel, ts=ts),
        grid=(B, S // ts),
        in_specs=[
            pl.BlockSpec((None, ts, D_MODEL), lambda b, s: (b, s, COL_XREC)),
            pl.BlockSpec((None, ts, D_MODEL), lambda b, s: (b, s, COL_YREC)),
            full((CONV_W, D_MODEL)),
            full((1, D_MODEL)),
            full((N_LRU_GROUPS, LRU_GROUP, 2 * LRU_GROUP)),
            full((1, D_MODEL)),
            full((1, D_MODEL)),
            full((1, D_MODEL)),
        ],
        out_specs=pl.BlockSpec((None, ts, D_MODEL), lambda b, s: (b, s, 0)),
        out_shape=jax.ShapeDtypeStruct((B, S, D_MODEL), jnp.bfloat16),
        scratch_shapes=[
            pltpu.VMEM((ts + 8, D_MODEL), jnp.float32),
            pltpu.VMEM((ts, D_MODEL), jnp.float32),
            pltpu.VMEM((ts, D_MODEL), jnp.float32),
            pltpu.VMEM((8, D_MODEL), jnp.float32),
        ],
        compiler_params=_cparams(("parallel", "arbitrary")),
        name="rglru",
    )(proj3, proj3, conv_w, conv_b, wg, ba, bx, lam)


def _merge_kernel(attn_ref, rec_ref, ga_ref, gr_ref, x_ref, wa_ref, wr_ref, wo_ref,
                  gba_ref, gbr_ref, n2_ref, rw_ref, rb_ref,
                  x1_ref, xn2_ref, idx_ref, gate_ref, rank_ref, cnt_ref, run_ref, *, tm):
    @pl.when(pl.program_id(0) == 0)
    def _():
        run_ref[...] = jnp.zeros_like(run_ref)

    pa = jnp.dot(attn_ref[...], wa_ref[...], preferred_element_type=jnp.float32)
    pr = jnp.dot(rec_ref[...], wr_ref[...], preferred_element_type=jnp.float32)
    g_a = jax.nn.sigmoid(ga_ref[...].astype(jnp.float32) + gba_ref[...])
    g_r = jax.nn.sigmoid(gr_ref[...].astype(jnp.float32) + gbr_ref[...])
    merged = (g_a * pa + g_r * pr).astype(jnp.bfloat16)
    x1 = x_ref[...] + jnp.dot(merged, wo_ref[...], preferred_element_type=jnp.float32)
    x1_ref[...] = x1
    xn2 = _rms(x1, n2_ref[...])
    xn2_ref[...] = xn2

    logits = jnp.dot(xn2, rw_ref[...], precision=lax.Precision.HIGHEST,
                     preferred_element_type=jnp.float32) + rb_ref[...]
    lane = lax.broadcasted_iota(jnp.int32, (tm, LANES), 1).astype(jnp.float32)
    idx_out = jnp.zeros((tm, LANES), jnp.float32)
    val_out = jnp.zeros((tm, LANES), jnp.float32)
    onehot = jnp.zeros((tm, LANES), jnp.float32)
    sel = []
    work = logits
    for j in range(TOP_K):
        m = jnp.max(work, axis=-1, keepdims=True)
        idx = jnp.min(jnp.where(work == m, lane, float(LANES)), axis=-1, keepdims=True)
        hit = lane == idx
        sel.append(hit)
        onehot = jnp.where(hit, 1.0, onehot)
        idx_out = jnp.where(lane == float(j), idx, idx_out)
        val_out = jnp.where(lane == float(j), m, val_out)
        work = jnp.where(hit, NEG_BIG, work)
    is_slot = lane < float(TOP_K)
    e = jnp.where(is_slot, jnp.exp(val_out - jnp.max(jnp.where(is_slot, val_out, NEG_BIG),
                                                      axis=-1, keepdims=True)), 0.0)
    gate_ref[...] = e / jnp.sum(e, axis=-1, keepdims=True)
    idx_ref[...] = idx_out.astype(jnp.int32)

    row = lax.broadcasted_iota(jnp.int32, (tm, tm), 0)
    col = lax.broadcasted_iota(jnp.int32, (tm, tm), 1)
    strict = (row > col).astype(jnp.bfloat16)
    before = jnp.dot(strict, onehot.astype(jnp.bfloat16),
                     preferred_element_type=jnp.float32) + run_ref[...]
    rank_out = jnp.zeros((tm, LANES), jnp.float32)
    for j in range(TOP_K):
        rj = jnp.sum(jnp.where(sel[j], before, 0.0), axis=-1, keepdims=True)
        rank_out = jnp.where(lane == float(j), rj, rank_out)
    rank_ref[...] = rank_out.astype(jnp.int32)
    run_ref[...] = run_ref[...] + jnp.sum(onehot, axis=0, keepdims=True)
    cnt_ref[...] = run_ref[...].astype(jnp.int32)


def _merge(attn2, rec2, proj2, x2, wa, wr, wo, gba, gbr, n2, rw, rb, tm):
    T = x2.shape[0]
    tok = lambda c: pl.BlockSpec((tm, D_MODEL), lambda i: (i, c))
    full = lambda shape: pl.BlockSpec(shape, lambda i: (0,) * len(shape))
    meta = pl.BlockSpec((tm, LANES), lambda i: (i, 0))
    return pl.pallas_call(
        functools.partial(_merge_kernel, tm=tm),
        grid=(T // tm,),
        in_specs=[tok(0), tok(0), tok(COL_GA), tok(COL_GR), tok(0),
                  full((D_MODEL, D_MODEL)), full((D_MODEL, D_MODEL)), full((D_MODEL, D_MODEL)),
                  full((1, D_MODEL)), full((1, D_MODEL)), full((1, D_MODEL)),
                  full((D_MODEL, LANES)), full((1, LANES))],
        out_specs=[tok(0), tok(0), meta, meta, meta, full((1, LANES))],
        out_shape=[
            jax.ShapeDtypeStruct((T, D_MODEL), jnp.float32),
            jax.ShapeDtypeStruct((T, D_MODEL), jnp.float32),
            jax.ShapeDtypeStruct((T, LANES), jnp.int32),
            jax.ShapeDtypeStruct((T, LANES), jnp.float32),
            jax.ShapeDtypeStruct((T, LANES), jnp.int32),
            jax.ShapeDtypeStruct((1, LANES), jnp.int32),
        ],
        scratch_shapes=[pltpu.VMEM((1, LANES), jnp.float32)],
        compiler_params=_cparams(("arbitrary",)),
        name="merge_router",
    )(attn2, rec2, proj2, proj2, x2, wa, wr, wo, gba, gbr, n2, rw, rb)


def _row_copy(src, src_row, dst, dst_row, sem):
    return pltpu.make_async_copy(src.at[pl.ds(src_row, 1), :], dst.at[pl.ds(dst_row, 1), :], sem)


def _dispatch_kernel(pos_ref, pad_ref, xn_ref, zero_ref, xs_ref, sem, *, tm):
    n_rows = tm * TOP_K

    @pl.when(pl.program_id(0) == 0)
    def _():
        def per_expert(e, carry):
            lo, hi = pad_ref[0, e], pad_ref[1, e]

            def fill(r, c):
                _row_copy(zero_ref, 0, xs_ref, r, sem).start()
                return c

            lax.fori_loop(lo, hi, fill, 0)

            def drain(r, c):
                _row_copy(zero_ref, 0, xs_ref, 0, sem).wait()
                return c

            lax.fori_loop(lo, hi, drain, 0)
            return carry

        lax.fori_loop(0, N_EXPERTS + 1, per_expert, 0)

    def start(r, c):
        _row_copy(xn_ref, r // TOP_K, xs_ref, pos_ref[0, 0, r], sem).start()
        return c

    lax.fori_loop(0, n_rows, start, 0)

    def drain(r, c):
        _row_copy(xn_ref, 0, xs_ref, 0, sem).wait()
        return c

    lax.fori_loop(0, n_rows, drain, 0)


def _dispatch(pos3, pad_rows, xn2, n_pad_rows, tm):
    T = xn2.shape[0]
    zero_row = jnp.zeros((8, D_MODEL), xn2.dtype)
    return pl.pallas_call(
        functools.partial(_dispatch_kernel, tm=tm),
        grid=(T // tm,),
        in_specs=[
            pl.BlockSpec((1, 1, tm * TOP_K), lambda i: (i, 0, 0), memory_space=pltpu.SMEM),
            pl.BlockSpec(memory_space=pltpu.SMEM),
            pl.BlockSpec((tm, D_MODEL), lambda i: (i, 0)),
            pl.BlockSpec((8, D_MODEL), lambda i: (0, 0)),
        ],
        out_specs=pl.BlockSpec(memory_space=pl.ANY),
        out_shape=jax.ShapeDtypeStruct((n_pad_rows, D_MODEL), xn2.dtype),
        scratch_shapes=[pltpu.SemaphoreType.DMA(())],
        compiler_params=_cparams(("arbitrary",)),
        name="dispatch",
    )(pos3, pad_rows, xn2, zero_row)


def _moe_kernel(be_ref, nu_ref, xs_ref, w1_ref, b1_ref, w2_ref, b2_ref, ys_ref, w1b_ref, w2b_ref):
    i = pl.program_id(0)
    prev = be_ref[jnp.maximum(i - 1, 0)]

    @pl.when((i == 0) | (be_ref[i] != prev))
    def _():
        w1b_ref[...] = w1_ref[...].astype(jnp.bfloat16)
        w2b_ref[...] = w2_ref[...].astype(jnp.bfloat16)

    @pl.when(i < nu_ref[0])
    def _():
        h = jnp.dot(xs_ref[...].astype(jnp.bfloat16), w1b_ref[...],
                    preferred_element_type=jnp.float32) + b1_ref[...]
        d = h.shape[1] // 2
        glu = jnp.minimum(h[:, :d], SWIGLU_LIMIT)
        lin = jnp.clip(h[:, d:], -SWIGLU_LIMIT, SWIGLU_LIMIT)
        act = glu * jax.nn.sigmoid(SWIGLU_ALPHA * glu) * (lin + 1.0)
        ys_ref[...] = jnp.dot(act.astype(jnp.bfloat16), w2b_ref[...],
                              preferred_element_type=jnp.float32) + b2_ref[...]

    @pl.when(i >= nu_ref[0])
    def _():
        ys_ref[...] = jnp.zeros_like(ys_ref)


def _moe(block_expert, n_used, xs, w1, b1, w2, b2, te):
    n_blocks = xs.shape[0] // te
    d_ff2 = w1.shape[2]
    grid_spec = pltpu.PrefetchScalarGridSpec(
        num_scalar_prefetch=2,
        grid=(n_blocks,),
        in_specs=[
            pl.BlockSpec((te, D_MODEL), lambda i, be, nu: (jnp.minimum(i, nu[0] - 1), 0)),
            pl.BlockSpec((None, D_MODEL, d_ff2), lambda i, be, nu: (be[i], 0, 0)),
            pl.BlockSpec((None, 1, d_ff2), lambda i, be, nu: (be[i], 0, 0)),
            pl.BlockSpec((None, d_ff2 // 2, D_MODEL), lambda i, be, nu: (be[i], 0, 0)),
            pl.BlockSpec((None, 1, D_MODEL), lambda i, be, nu: (be[i], 0, 0)),
        ],
        out_specs=pl.BlockSpec((te, D_MODEL), lambda i, be, nu: (i, 0)),
        scratch_shapes=[pltpu.VMEM((D_MODEL, d_ff2), jnp.bfloat16),
                        pltpu.VMEM((d_ff2 // 2, D_MODEL), jnp.bfloat16)],
    )
    return pl.pallas_call(
        _moe_kernel,
        grid_spec=grid_spec,
        out_shape=jax.ShapeDtypeStruct(xs.shape, jnp.float32),
        compiler_params=_cparams(("arbitrary",)),
        name="moe_experts",
    )(block_expert, n_used, xs, w1, b1, w2, b2)


def _combine_kernel(pos_ref, gate_ref, x1_ref, fs_ref, ys_ref, o_ref, buf_ref, sem, *, tc):
    n_rows = tc * TOP_K

    def start(r, c):
        t, j = r // TOP_K, r % TOP_K
        pltpu.make_async_copy(ys_ref.at[pl.ds(pos_ref[0, 0, r], 1), :],
                              buf_ref.at[j, pl.ds(t, 1), :], sem).start()
        return c

    lax.fori_loop(0, n_rows, start, 0)

    def drain(r, c):
        pltpu.make_async_copy(ys_ref.at[pl.ds(0, 1), :], buf_ref.at[0, pl.ds(0, 1), :], sem).wait()
        return c

    lax.fori_loop(0, n_rows, drain, 0)

    gates = gate_ref[...]
    acc = x1_ref[...]
    for j in range(TOP_K):
        acc = acc + gates[:, j:j + 1] * buf_ref[j]
    o_ref[...] = _rms(acc, fs_ref[...])


def _combine(pos3, gates, x1, fscale, ys, tc):
    T = x1.shape[0]
    return pl.pallas_call(
        functools.partial(_combine_kernel, tc=tc),
        grid=(T // tc,),
        in_specs=[
            pl.BlockSpec((1, 1, tc * TOP_K), lambda i: (i, 0, 0), memory_space=pltpu.SMEM),
            pl.BlockSpec((tc, LANES), lambda i: (i, 0)),
            pl.BlockSpec((tc, D_MODEL), lambda i: (i, 0)),
            pl.BlockSpec((1, D_MODEL), lambda i: (0, 0)),
            pl.BlockSpec(memory_space=pl.ANY),
        ],
        out_specs=pl.BlockSpec((tc, D_MODEL), lambda i: (i, 0)),
        out_shape=jax.ShapeDtypeStruct((T, D_MODEL), jnp.float32),
        scratch_shapes=[pltpu.VMEM((TOP_K, tc, D_MODEL), jnp.float32),
                        pltpu.SemaphoreType.DMA(())],
        compiler_params=_cparams(("arbitrary",)),
        name="combine",
    )(pos3, gates, x1, fscale, ys)


def _block_diag_groups(wa, wx):
    per = LRU_GROUP // LRU_BLOCK

    def bd(w):
        w = w.reshape(N_LRU_GROUPS, per, LRU_BLOCK, LRU_BLOCK)
        eye = jnp.eye(per, dtype=w.dtype)
        return jnp.einsum('gnij,nm->gnimj', w, eye).reshape(N_LRU_GROUPS, LRU_GROUP, LRU_GROUP)

    return jnp.concatenate([bd(wa), bd(wx)], axis=-1).astype(jnp.bfloat16)


def _pick_tile(n, pref):
    t = min(pref, n)
    while n % t:
        t //= 2
    return t


def _layer(x, norm1_scale, w_in, f_bias, gate_bias, conv_w, conv_b, rg_wa, rg_ba, rg_wx, rg_bx,
           rg_lambda, w_attn_proj, w_rec_proj, w_out, norm2_scale, router_w, router_b,
           moe_w1, moe_b1, moe_w2, moe_b2, final_scale):
    B, S, D = x.shape
    T = B * S
    A = T * TOP_K
    x2 = x.reshape(T, D)
    row = lambda v: v.reshape(1, -1).astype(jnp.float32)

    aw = N_HEADS * HEAD_DIM
    o_f = 3 * aw
    o_x = o_f + N_HEADS
    w_main = jnp.concatenate([w_in[:, :aw] * (HEAD_DIM ** -0.5), w_in[:, aw:o_f], w_in[:, o_x:]],
                             axis=1).astype(jnp.bfloat16)
    w_f = jnp.pad(w_in[:, o_f:o_x], ((0, 0), (0, LANES - N_HEADS))).astype(jnp.bfloat16)
    f_b = jnp.pad(row(f_bias), ((0, 0), (0, LANES - N_HEADS)))

    proj, f_logit = _in_proj(x2, row(norm1_scale), w_main, w_f, _pick_tile(T, 1024))
    proj3 = proj.reshape(B, S, PROJ_COLS)

    c = _fgate(f_logit.reshape(B, S, LANES), f_b, chunk=_pick_tile(S, 256))
    attn = _attention(proj3, c.reshape(B, N_SLABS, HEADS_PER_SLAB, S), _pick_tile(S, 256))

    rec = _rglru(proj3, conv_w.astype(jnp.float32), row(conv_b), _block_diag_groups(rg_wa, rg_wx),
                 row(rg_ba), row(rg_bx), row(rg_lambda), _pick_tile(S, 512))

    rw = jnp.pad(router_w.astype(jnp.float32), ((0, 0), (0, LANES - N_EXPERTS)))
    rb = jnp.pad(row(router_b), ((0, 0), (0, LANES - N_EXPERTS)), constant_values=NEG_BIG)
    gb = row(gate_bias)
    tm = _pick_tile(T, 512)
    x1, xn2, top_idx, gates, rank, counts = _merge(
        attn.reshape(T, D), rec.reshape(T, D), proj, x2,
        w_attn_proj.astype(jnp.bfloat16), w_rec_proj.astype(jnp.bfloat16), w_out.astype(jnp.bfloat16),
        gb[:, :D], gb[:, D:], row(norm2_scale), rw, rb, tm)

    te = 512
    n_blocks = -(-A // te) + N_EXPERTS
    counts = counts[0, :N_EXPERTS]
    padded = (counts + te - 1) // te * te
    pad_end = jnp.cumsum(padded)
    pad_start = pad_end - padded
    experts = jnp.arange(N_EXPERTS, dtype=jnp.int32)
    sel = top_idx[:, :TOP_K, None] == experts
    pos = (jnp.sum(jnp.where(sel, pad_start, 0), axis=-1) + rank[:, :TOP_K]).astype(jnp.int32)
    n_used = (pad_end[-1] // te).astype(jnp.int32)
    blk = jnp.arange(n_blocks, dtype=jnp.int32)
    be = jnp.sum(pad_end[None, :] <= (blk * te)[:, None], axis=-1)
    block_expert = jnp.minimum(jnp.minimum(be, be[jnp.maximum(n_used - 1, 0)]),
                               N_EXPERTS - 1).astype(jnp.int32)
    n_pad_rows = n_blocks * te
    fill_lo = jnp.concatenate([pad_start + counts, pad_end[-1:]])
    fill_hi = jnp.concatenate([pad_end, jnp.full((1,), n_pad_rows, pad_end.dtype)])
    pad_rows = jnp.stack([fill_lo, fill_hi]).astype(jnp.int32)

    xs = _dispatch(pos.reshape(T // tm, 1, tm * TOP_K), pad_rows, xn2, n_pad_rows, tm)
    ys = _moe(block_expert, n_used.reshape(1), xs, moe_w1, moe_b1.reshape(N_EXPERTS, 1, -1),
              moe_w2, moe_b2.reshape(N_EXPERTS, 1, -1), te)
    tc = _pick_tile(T, 256)
    out = _combine(pos.reshape(T // tc, 1, tc * TOP_K), gates, x1, row(final_scale), ys, tc)
    return out.reshape(B, S, D)


def kernel(x, norm1_scale, w_in, f_bias, gate_bias, conv_w, conv_b, rg_wa, rg_ba, rg_wx, rg_bx, rg_lambda, w_attn_proj, w_rec_proj, w_out, norm2_scale, router_w, router_b, moe_w1, moe_b1, moe_w2, moe_b2, final_norm_scale):
    depth = norm1_scale.shape[0]
    assert depth == 1, "the final rmsnorm is fused into the single layer's combine step"
    l = 0
    return _layer(x, norm1_scale[l], w_in[l], f_bias[l], gate_bias[l], conv_w[l], conv_b[l],
                  rg_wa[l], rg_ba[l], rg_wx[l], rg_bx[l], rg_lambda[l], w_attn_proj[l],
                  w_rec_proj[l], w_out[l], norm2_scale[l], router_w[l], router_b[l],
                  moe_w1[l], moe_b1[l], moe_w2[l], moe_b2[l], final_norm_scale)
```

```python
import functools

import jax
import jax.numpy as jnp
from jax import lax
from jax.experimental import pallas as pl
from jax.experimental.pallas import tpu as pltpu

D_MODEL = 1024
N_HEADS = 16
HEAD_DIM = 64
LANES = 128
HEADS_PER_SLAB = LANES // HEAD_DIM
N_SLABS = D_MODEL // LANES
N_LRU_BLOCKS = 16
LRU_BLOCK = 64
LRU_GROUP = 256
N_LRU_GROUPS = D_MODEL // LRU_GROUP
CONV_W = 4
LRU_C = 8.0
N_EXPERTS = 32
TOP_K = 4
SWIGLU_ALPHA = 1.702
SWIGLU_LIMIT = 7.0
RMS_EPS = 1e-6
NEG_BIG = -1e30
LOG2E = 1.4426950408889634

PROJ_COLS = 7 * D_MODEL
COL_Q, COL_K, COL_V, COL_XREC, COL_YREC, COL_GA, COL_GR = range(7)

VMEM_LIMIT = 56 * 1024 * 1024


def _cparams(sem):
    return pltpu.CompilerParams(dimension_semantics=sem, vmem_limit_bytes=VMEM_LIMIT)


def _softplus(z):
    return jnp.maximum(z, 0.0) + jnp.log1p(jnp.exp(-jnp.abs(z)))


def _rms(x, scale):
    return x * lax.rsqrt(jnp.mean(x * x, axis=-1, keepdims=True) + RMS_EPS) * scale


def _in_proj_kernel(x_ref, scale_ref, w_ref, wf_ref, proj_ref, f_ref, xn_ref):
    @pl.when(pl.program_id(1) == 0)
    def _():
        xn = _rms(x_ref[...], scale_ref[...]).astype(jnp.bfloat16)
        xn_ref[...] = xn
        f_ref[...] = jnp.dot(xn, wf_ref[...], preferred_element_type=jnp.float32)

    proj_ref[...] = jnp.dot(xn_ref[...], w_ref[...],
                            preferred_element_type=jnp.float32).astype(proj_ref.dtype)


def _in_proj(x2, scale, w, wf, tm):
    T = x2.shape[0]
    n_col = w.shape[1] // D_MODEL
    return pl.pallas_call(
        _in_proj_kernel,
        grid=(T // tm, n_col),
        in_specs=[
            pl.BlockSpec((tm, D_MODEL), lambda i, j: (i, 0)),
            pl.BlockSpec((1, D_MODEL), lambda i, j: (0, 0)),
            pl.BlockSpec((D_MODEL, D_MODEL), lambda i, j: (0, j)),
            pl.BlockSpec((D_MODEL, LANES), lambda i, j: (0, 0)),
        ],
        out_specs=[
            pl.BlockSpec((tm, D_MODEL), lambda i, j: (i, j)),
            pl.BlockSpec((tm, LANES), lambda i, j: (i, 0)),
        ],
        out_shape=[
            jax.ShapeDtypeStruct((T, w.shape[1]), jnp.bfloat16),
            jax.ShapeDtypeStruct((T, LANES), jnp.float32),
        ],
        scratch_shapes=[pltpu.VMEM((tm, D_MODEL), jnp.bfloat16)],
        compiler_params=_cparams(("parallel", "arbitrary")),
        name="in_proj",
    )(x2, scale, w, wf)


def _fgate_kernel(f_ref, fb_ref, c_ref, *, chunk):
    S = f_ref.shape[0]
    row = lax.broadcasted_iota(jnp.int32, (chunk, chunk), 0)
    col = lax.broadcasted_iota(jnp.int32, (chunk, chunk), 1)
    tri = (row >= col).astype(jnp.float32)
    carry = jnp.zeros((1, LANES), jnp.float32)
    for c in range(S // chunk):
        z = f_ref[pl.ds(c * chunk, chunk), :] + fb_ref[...]
        log_f = -_softplus(-z)
        cs = jnp.dot(tri, log_f, precision=lax.Precision.HIGHEST,
                     preferred_element_type=jnp.float32) + carry
        carry = cs[chunk - 1:chunk, :]
        c_ref[:, pl.ds(c * chunk, chunk)] = (cs * LOG2E).T[:N_HEADS, :]


def _fgate(f3, fb, chunk=256):
    B, S, _ = f3.shape
    return pl.pallas_call(
        functools.partial(_fgate_kernel, chunk=chunk),
        grid=(B,),
        in_specs=[
            pl.BlockSpec((None, S, LANES), lambda b: (b, 0, 0)),
            pl.BlockSpec((1, LANES), lambda b: (0, 0)),
        ],
        out_specs=pl.BlockSpec((None, N_HEADS, S), lambda b: (b, 0, 0)),
        out_shape=jax.ShapeDtypeStruct((B, N_HEADS, S), jnp.float32),
        compiler_params=_cparams(("parallel",)),
        name="fgate",
    )(f3, fb)


def _attn_kernel(q_ref, k_ref, v_ref, c_ref, o_ref, *, tq, tk):
    S = q_ref.shape[0]
    lane = lax.broadcasted_iota(jnp.int32, (1, LANES), 1)
    head0 = lane < HEAD_DIM
    row = lax.broadcasted_iota(jnp.int32, (HEADS_PER_SLAB * tq, tk), 0) % tq
    col = lax.broadcasted_iota(jnp.int32, (HEADS_PER_SLAB * tq, tk), 1)

    def chunk(ks, carry, q2, mask):
        m, l, acc = carry
        s = lax.dot_general(q2, k_ref[pl.ds(ks, tk), :], (((1,), (1,)), ((), ())),
                            preferred_element_type=jnp.float32)
        cb = c_ref[:, pl.ds(ks, tk)]
        s = jnp.concatenate([s[:tq] - cb[0:1], s[tq:] - cb[1:2]], axis=0)
        if mask is not None:
            s = jnp.where(mask, s, NEG_BIG)
        m_new = jnp.maximum(m, jnp.max(s, axis=-1, keepdims=True))
        alpha = jnp.exp2(m - m_new)
        p = jnp.exp2(s - m_new)
        l = alpha * l + jnp.sum(p, axis=-1, keepdims=True)
        pv = jnp.dot(p.astype(jnp.bfloat16), v_ref[pl.ds(ks, tk), :],
                     preferred_element_type=jnp.float32)
        return m_new, l, alpha * acc + pv

    for qi in range(S // tq):
        qs = qi * tq
        q = q_ref[pl.ds(qs, tq), :]
        zero = jnp.zeros_like(q)
        q2 = jnp.concatenate([jnp.where(head0, q, zero), jnp.where(head0, zero, q)], axis=0)
        carry = (jnp.full((HEADS_PER_SLAB * tq, 1), NEG_BIG, jnp.float32),
                 jnp.zeros((HEADS_PER_SLAB * tq, 1), jnp.float32),
                 jnp.zeros((HEADS_PER_SLAB * tq, LANES), jnp.float32))
        n_full = qs // tk
        n_chunks = -(-(qs + tq) // tk)
        for kc in range(n_chunks):
            mask = None if kc < n_full else (row + qs) >= (col + kc * tk)
            carry = chunk(kc * tk, carry, q2, mask)
        _, l, acc = carry
        out = acc / l
        o_ref[pl.ds(qs, tq), :] = jnp.where(head0, out[:tq], out[tq:]).astype(o_ref.dtype)


def _attention(proj3, c4, tq, tk):
    B, S, _ = proj3.shape
    slab = lambda c: pl.BlockSpec((None, S, LANES), lambda b, p: (b, 0, c * N_SLABS + p))
    return pl.pallas_call(
        functools.partial(_attn_kernel, tq=tq, tk=tk),
        grid=(B, N_SLABS),
        in_specs=[slab(COL_Q), slab(COL_K), slab(COL_V),
                  pl.BlockSpec((None, None, HEADS_PER_SLAB, S), lambda b, p: (b, p, 0, 0))],
        out_specs=pl.BlockSpec((None, S, LANES), lambda b, p: (b, 0, p)),
        out_shape=jax.ShapeDtypeStruct((B, S, D_MODEL), jnp.bfloat16),
        compiler_params=_cparams(("parallel", "parallel")),
        name="attn",
    )(proj3, proj3, proj3, c4)


def _rglru_kernel(x_ref, y_ref, cw_ref, cb_ref, wg_ref, ba_ref, bx_ref, lam_ref, o_ref,
                  xe_ref, a_ref, b_ref, h_ref, *, ts):
    HALO = 8
    s_idx = pl.program_id(1)

    @pl.when(s_idx == 0)
    def _():
        xe_ref[pl.ds(0, HALO), :] = jnp.zeros((HALO, D_MODEL), jnp.float32)
        h_ref[...] = jnp.zeros_like(h_ref)

    xe_ref[pl.ds(HALO, ts), :] = x_ref[...].astype(jnp.float32)
    conv = cb_ref[...] + cw_ref[CONV_W - 1:CONV_W, :] * xe_ref[pl.ds(HALO, ts), :]
    for j in range(CONV_W - 1):
        shift = CONV_W - 1 - j
        conv = conv + cw_ref[j:j + 1, :] * xe_ref[pl.ds(HALO - shift, ts), :]
    xe_ref[pl.ds(0, HALO), :] = xe_ref[pl.ds(ts, HALO), :]

    cb16 = conv.astype(jnp.bfloat16)
    sp = _softplus(-lam_ref[...])
    first = (lax.broadcasted_iota(jnp.int32, (ts, 1), 0) + s_idx * ts) == 0
    for g in range(N_LRU_GROUPS):
        cols = slice(g * LRU_GROUP, (g + 1) * LRU_GROUP)
        gates = jnp.dot(cb16[:, cols], wg_ref[g], preferred_element_type=jnp.float32)
        r = jax.nn.sigmoid(gates[:, :LRU_GROUP] + ba_ref[:, cols])
        i = jax.nn.sigmoid(gates[:, LRU_GROUP:] + bx_ref[:, cols])
        log_a = -LRU_C * r * sp[:, cols]
        a = jnp.exp(log_a)
        mult = jnp.where(first, 1.0, jnp.sqrt(1.0 - a * a))
        a_ref[:, cols] = a
        b_ref[:, cols] = mult * (i * conv[:, cols])

    SUB = 8
    rows = lax.broadcasted_iota(jnp.int32, (SUB, D_MODEL), 0)

    def slab(i, h_prev):
        start = pl.multiple_of(i * SUB, SUB)
        a = a_ref[pl.ds(start, SUB), :]
        b = b_ref[pl.ds(start, SUB), :]
        for d in (1, 2, 4):
            keep = rows >= d
            a_s = jnp.where(keep, pltpu.roll(a, d, 0), 1.0)
            b_s = jnp.where(keep, pltpu.roll(b, d, 0), 0.0)
            b = a * b_s + b
            a = a * a_s
        h = a * h_prev + b
        gate = jax.nn.gelu(y_ref[pl.ds(start, SUB), :].astype(jnp.float32))
        o_ref[pl.ds(start, SUB), :] = (h * gate).astype(o_ref.dtype)
        return jnp.broadcast_to(h[SUB - 1:SUB, :], (SUB, D_MODEL))

    h_ref[...] = lax.fori_loop(0, ts // SUB, slab, h_ref[...], unroll=4)


def _rglru(proj3, conv_w, conv_b, wg, ba, bx, lam, ts):
    B, S, _ = proj3.shape
    full = lambda shape: pl.BlockSpec(shape, lambda b, s: (0,) * len(shape))
    return pl.pallas_call(
        functools.partial(_rglru_kernel, ts=ts),
        grid=(B, S // ts),
        in_specs=[
            pl.BlockSpec((None, ts, D_MODEL), lambda b, s: (b, s, COL_XREC)),
            pl.BlockSpec((None, ts, D_MODEL), lambda b, s: (b, s, COL_YREC)),
            full((CONV_W, D_MODEL)),
            full((1, D_MODEL)),
            full((N_LRU_GROUPS, LRU_GROUP, 2 * LRU_GROUP)),
            full((1, D_MODEL)),
            full((1, D_MODEL)),
            full((1, D_MODEL)),
        ],
        out_specs=pl.BlockSpec((None, ts, D_MODEL), lambda b, s: (b, s, 0)),
        out_shape=jax.ShapeDtypeStruct((B, S, D_MODEL), jnp.bfloat16),
        scratch_shapes=[
            pltpu.VMEM((ts + 8, D_MODEL), jnp.float32),
            pltpu.VMEM((ts, D_MODEL), jnp.float32),
            pltpu.VMEM((ts, D_MODEL), jnp.float32),
            pltpu.VMEM((8, D_MODEL), jnp.float32),
        ],
        compiler_params=_cparams(("parallel", "arbitrary")),
        name="rglru",
    )(proj3, proj3, conv_w, conv_b, wg, ba, bx, lam)


def _merge_kernel(attn_ref, rec_ref, ga_ref, gr_ref, x_ref, wa_ref, wr_ref, wo_ref,
                  gba_ref, gbr_ref, n2_ref, rw_ref, rb_ref,
                  x1_ref, xn2_ref, idx_ref, gate_ref, rank_ref, cnt_ref, *, tm):
    pa = jnp.dot(attn_ref[...], wa_ref[...], preferred_element_type=jnp.float32)
    pr = jnp.dot(rec_ref[...], wr_ref[...], preferred_element_type=jnp.float32)
    g_a = jax.nn.sigmoid(ga_ref[...].astype(jnp.float32) + gba_ref[...])
    g_r = jax.nn.sigmoid(gr_ref[...].astype(jnp.float32) + gbr_ref[...])
    merged = (g_a * pa + g_r * pr).astype(jnp.bfloat16)
    x1 = x_ref[...] + jnp.dot(merged, wo_ref[...], preferred_element_type=jnp.float32)
    x1_ref[...] = x1
    xn2 = _rms(x1, n2_ref[...])
    xn2_ref[...] = xn2.astype(xn2_ref.dtype)

    logits = jnp.dot(xn2, rw_ref[...], precision=lax.Precision.HIGHEST,
                     preferred_element_type=jnp.float32) + rb_ref[...]
    lane = lax.broadcasted_iota(jnp.int32, (tm, LANES), 1).astype(jnp.float32)
    idx_out = jnp.zeros((tm, LANES), jnp.float32)
    val_out = jnp.zeros((tm, LANES), jnp.float32)
    onehot = jnp.zeros((tm, LANES), jnp.float32)
    sel = []
    work = logits
    for j in range(TOP_K):
        m = jnp.max(work, axis=-1, keepdims=True)
        idx = jnp.min(jnp.where(work == m, lane, float(LANES)), axis=-1, keepdims=True)
        hit = lane == idx
        sel.append(hit)
        onehot = jnp.where(hit, 1.0, onehot)
        idx_out = jnp.where(lane == float(j), idx, idx_out)
        val_out = jnp.where(lane == float(j), m, val_out)
        work = jnp.where(hit, NEG_BIG, work)
    is_slot = lane < float(TOP_K)
    e = jnp.where(is_slot, jnp.exp(val_out - jnp.max(jnp.where(is_slot, val_out, NEG_BIG),
                                                      axis=-1, keepdims=True)), 0.0)
    gate_ref[...] = e / jnp.sum(e, axis=-1, keepdims=True)
    idx_ref[...] = idx_out.astype(jnp.int32)

    row = lax.broadcasted_iota(jnp.int32, (tm, tm), 0)
    col = lax.broadcasted_iota(jnp.int32, (tm, tm), 1)
    strict = (row > col).astype(jnp.bfloat16)
    before = jnp.dot(strict, onehot.astype(jnp.bfloat16), preferred_element_type=jnp.float32)
    rank_out = jnp.zeros((tm, LANES), jnp.float32)
    for j in range(TOP_K):
        rj = jnp.sum(jnp.where(sel[j], before, 0.0), axis=-1, keepdims=True)
        rank_out = jnp.where(lane == float(j), rj, rank_out)
    rank_ref[...] = rank_out.astype(jnp.int32)
    cnt_ref[...] = jnp.sum(onehot, axis=0, keepdims=True).astype(jnp.int32)


def _merge(attn2, rec2, proj2, x2, wa, wr, wo, gba, gbr, n2, rw, rb, tm):
    T = x2.shape[0]
    tok = lambda c: pl.BlockSpec((tm, D_MODEL), lambda i: (i, c))
    full = lambda shape: pl.BlockSpec(shape, lambda i: (0,) * len(shape))
    meta = pl.BlockSpec((tm, LANES), lambda i: (i, 0))
    return pl.pallas_call(
        functools.partial(_merge_kernel, tm=tm),
        grid=(T // tm,),
        in_specs=[tok(0), tok(0), tok(COL_GA), tok(COL_GR), tok(0),
                  full((D_MODEL, D_MODEL)), full((D_MODEL, D_MODEL)), full((D_MODEL, D_MODEL)),
                  full((1, D_MODEL)), full((1, D_MODEL)), full((1, D_MODEL)),
                  full((D_MODEL, LANES)), full((1, LANES))],
        out_specs=[tok(0), tok(0), meta, meta, meta,
                   pl.BlockSpec((None, 1, LANES), lambda i: (i, 0, 0))],
        out_shape=[
            jax.ShapeDtypeStruct((T, D_MODEL), jnp.float32),
            jax.ShapeDtypeStruct((T, D_MODEL), jnp.bfloat16),
            jax.ShapeDtypeStruct((T, LANES), jnp.int32),
            jax.ShapeDtypeStruct((T, LANES), jnp.float32),
            jax.ShapeDtypeStruct((T, LANES), jnp.int32),
            jax.ShapeDtypeStruct((T // tm, 1, LANES), jnp.int32),
        ],
        compiler_params=_cparams(("parallel",)),
        name="merge_router",
    )(attn2, rec2, proj2, proj2, x2, wa, wr, wo, gba, gbr, n2, rw, rb)


SEG = 16
SEG_PIECES = 6
FILL_ROWS = SEG << (SEG_PIECES - 1)


def _segment_copies(src, src_off, dst, dst_off, n_seg, sem, wait):
    for k in range(SEG_PIECES):
        @pl.when(((n_seg >> k) & 1) == 1)
        def _(k=k):
            off = (n_seg & ((1 << k) - 1)) * SEG
            cp = pltpu.make_async_copy(
                src.at[pl.ds(pl.multiple_of(src_off + off, SEG), SEG << k), :],
                dst.at[pl.ds(pl.multiple_of(dst_off + off, SEG), SEG << k), :], sem)
            if wait:
                cp.wait()
            else:
                cp.start()


def _local_positions(idx_ref, rank_ref, lss_ref):
    tm = idx_ref.shape[0]
    lane = lax.broadcasted_iota(jnp.int32, (tm, LANES), 1)
    idx = idx_ref[...]
    lss = lss_ref[...].astype(jnp.float32)
    pos = jnp.zeros((tm, LANES), jnp.float32)
    for j in range(TOP_K):
        start = jnp.sum(jnp.where(lane == idx[:, j:j + 1], lss, 0.0), axis=-1, keepdims=True)
        pos = jnp.where(lane == j, start, pos)
    return pos + rank_ref[...].astype(jnp.float32)


def _dispatch_kernel(nseg_ref, lss_s_ref, gseg_ref, fill_ref, xn_ref, idx_ref, rank_ref, lss_ref,
                     zero_ref, xs_ref, buf_ref, sem, *, rc):
    i = pl.program_id(0)
    tm = xn_ref.shape[0]
    r_loc = buf_ref.shape[0]

    @pl.when(i == 0)
    def _():
        def fill_range(e, carry):
            lo = fill_ref[0, e]
            n = fill_ref[1, e] - lo
            n_big = n // FILL_ROWS
            rem = (n - n_big * FILL_ROWS) // SEG

            def big(b, wait):
                cp = pltpu.make_async_copy(
                    zero_ref, xs_ref.at[pl.ds(pl.multiple_of(lo + b * FILL_ROWS, SEG), FILL_ROWS), :], sem)
                if wait:
                    cp.wait()
                else:
                    cp.start()

            lax.fori_loop(0, n_big, lambda b, c: (big(b, False), c)[1], 0)
            _segment_copies(zero_ref, 0, xs_ref, lo + n_big * FILL_ROWS, rem, sem, False)
            lax.fori_loop(0, n_big, lambda b, c: (big(b, True), c)[1], 0)
            _segment_copies(zero_ref, 0, xs_ref, lo + n_big * FILL_ROWS, rem, sem, True)
            return carry

        lax.fori_loop(0, N_EXPERTS + 1, fill_range, 0)

    pos_t = _local_positions(idx_ref, rank_ref, lss_ref).T[:TOP_K, :]
    x = xn_ref[...]
    for c in range(r_loc // rc):
        rho = (lax.broadcasted_iota(jnp.int32, (rc, tm), 0) + c * rc).astype(jnp.float32)
        hit = rho == pos_t[0:1, :]
        for j in range(1, TOP_K):
            hit = hit | (rho == pos_t[j:j + 1, :])
        perm = jnp.where(hit, 1.0, 0.0).astype(jnp.bfloat16)
        buf_ref[pl.ds(c * rc, rc), :] = jnp.dot(
            perm, x, preferred_element_type=jnp.float32).astype(buf_ref.dtype)

    def per_expert(wait):
        def body(e, carry):
            k = i * N_EXPERTS + e
            _segment_copies(buf_ref, lss_s_ref[k], xs_ref, gseg_ref[k], nseg_ref[k], sem, wait)
            return carry
        return body

    lax.fori_loop(0, N_EXPERTS, per_expert(False), 0)
    lax.fori_loop(0, N_EXPERTS, per_expert(True), 0)


def _dispatch(nseg, lss_flat, gseg, fill, xn2, idx, rank, lss3, n_rows, tm, r_loc):
    T = xn2.shape[0]
    zeros = jnp.zeros((FILL_ROWS, D_MODEL), xn2.dtype)
    smem = pl.BlockSpec(memory_space=pltpu.SMEM)
    meta = pl.BlockSpec((tm, LANES), lambda i: (i, 0))
    return pl.pallas_call(
        functools.partial(_dispatch_kernel, rc=256),
        grid=(T // tm,),
        in_specs=[smem, smem, smem, smem,
                  pl.BlockSpec((tm, D_MODEL), lambda i: (i, 0)), meta, meta,
                  pl.BlockSpec((None, 1, LANES), lambda i: (i, 0, 0)),
                  pl.BlockSpec((FILL_ROWS, D_MODEL), lambda i: (0, 0))],
        out_specs=pl.BlockSpec(memory_space=pl.ANY),
        out_shape=jax.ShapeDtypeStruct((n_rows, D_MODEL), xn2.dtype),
        scratch_shapes=[pltpu.VMEM((r_loc, D_MODEL), xn2.dtype), pltpu.SemaphoreType.DMA(())],
        compiler_params=_cparams(("arbitrary",)),
        name="dispatch",
    )(nseg, lss_flat, gseg, fill, xn2, idx, rank, lss3, zeros)


def _moe_kernel(be_ref, nu_ref, xs_ref, w1_ref, b1_ref, w2_ref, b2_ref, ys_ref, w1b_ref, w2b_ref):
    i = pl.program_id(0)
    prev = be_ref[jnp.maximum(i - 1, 0)]

    @pl.when((i == 0) | (be_ref[i] != prev))
    def _():
        w1b_ref[...] = w1_ref[...].astype(jnp.bfloat16)
        w2b_ref[...] = w2_ref[...].astype(jnp.bfloat16)

    @pl.when(i < nu_ref[0])
    def _():
        h = jnp.dot(xs_ref[...], w1b_ref[...],
                    preferred_element_type=jnp.float32) + b1_ref[...]
        d = h.shape[1] // 2
        glu = jnp.minimum(h[:, :d], SWIGLU_LIMIT)
        lin = jnp.clip(h[:, d:], -SWIGLU_LIMIT, SWIGLU_LIMIT)
        act = glu * jax.nn.sigmoid(SWIGLU_ALPHA * glu) * (lin + 1.0)
        y = jnp.dot(act.astype(jnp.bfloat16), w2b_ref[...],
                    preferred_element_type=jnp.float32) + b2_ref[...]
        ys_ref[...] = y.astype(ys_ref.dtype)

    @pl.when(i >= nu_ref[0])
    def _():
        ys_ref[...] = jnp.zeros_like(ys_ref)


def _moe(block_expert, n_used, xs, w1, b1, w2, b2, te):
    n_blocks = xs.shape[0] // te
    d_ff2 = w1.shape[2]
    grid_spec = pltpu.PrefetchScalarGridSpec(
        num_scalar_prefetch=2,
        grid=(n_blocks,),
        in_specs=[
            pl.BlockSpec((te, D_MODEL), lambda i, be, nu: (jnp.minimum(i, nu[0] - 1), 0)),
            pl.BlockSpec((None, D_MODEL, d_ff2), lambda i, be, nu: (be[i], 0, 0)),
            pl.BlockSpec((None, 1, d_ff2), lambda i, be, nu: (be[i], 0, 0)),
            pl.BlockSpec((None, d_ff2 // 2, D_MODEL), lambda i, be, nu: (be[i], 0, 0)),
            pl.BlockSpec((None, 1, D_MODEL), lambda i, be, nu: (be[i], 0, 0)),
        ],
        out_specs=pl.BlockSpec((te, D_MODEL), lambda i, be, nu: (i, 0)),
        scratch_shapes=[pltpu.VMEM((D_MODEL, d_ff2), jnp.bfloat16),
                        pltpu.VMEM((d_ff2 // 2, D_MODEL), jnp.bfloat16)],
    )
    return pl.pallas_call(
        _moe_kernel,
        grid_spec=grid_spec,
        out_shape=jax.ShapeDtypeStruct(xs.shape, xs.dtype),
        compiler_params=_cparams(("arbitrary",)),
        name="moe_experts",
    )(block_expert, n_used, xs, w1, b1, w2, b2)


def _combine_kernel(nseg_ref, lss_s_ref, gseg_ref, idx_ref, rank_ref, lss_ref, gate_ref, x1_ref,
                    fs_ref, ys_ref, o_ref, buf_ref, sem, *, rc):
    i = pl.program_id(0)
    tm = x1_ref.shape[0]
    r_loc = buf_ref.shape[0]

    @pl.when(i == 0)
    def _():
        buf_ref[...] = jnp.zeros_like(buf_ref)

    def per_expert(wait):
        def body(e, carry):
            k = i * N_EXPERTS + e
            _segment_copies(ys_ref, gseg_ref[k], buf_ref, lss_s_ref[k], nseg_ref[k], sem, wait)
            return carry
        return body

    lax.fori_loop(0, N_EXPERTS, per_expert(False), 0)
    lax.fori_loop(0, N_EXPERTS, per_expert(True), 0)

    pos = _local_positions(idx_ref, rank_ref, lss_ref)
    gates = gate_ref[...]
    acc = x1_ref[...]
    for c in range(r_loc // rc):
        rho = (lax.broadcasted_iota(jnp.int32, (tm, rc), 1) + c * rc).astype(jnp.float32)
        g = jnp.zeros((tm, rc), jnp.float32)
        for j in range(TOP_K):
            g = g + jnp.where(rho == pos[:, j:j + 1], gates[:, j:j + 1], 0.0)
        acc = acc + jnp.dot(g.astype(jnp.bfloat16), buf_ref[pl.ds(c * rc, rc), :],
                            preferred_element_type=jnp.float32)
    o_ref[...] = _rms(acc, fs_ref[...])


def _combine(nseg, lss_flat, gseg, idx, rank, lss3, gates, x1, fscale, ys, tm, r_loc):
    T = x1.shape[0]
    smem = pl.BlockSpec(memory_space=pltpu.SMEM)
    meta = pl.BlockSpec((tm, LANES), lambda i: (i, 0))
    return pl.pallas_call(
        functools.partial(_combine_kernel, rc=256),
        grid=(T // tm,),
        in_specs=[smem, smem, smem, meta, meta,
                  pl.BlockSpec((None, 1, LANES), lambda i: (i, 0, 0)), meta,
                  pl.BlockSpec((tm, D_MODEL), lambda i: (i, 0)),
                  pl.BlockSpec((1, D_MODEL), lambda i: (0, 0)),
                  pl.BlockSpec(memory_space=pl.ANY)],
        out_specs=pl.BlockSpec((tm, D_MODEL), lambda i: (i, 0)),
        out_shape=jax.ShapeDtypeStruct((T, D_MODEL), jnp.float32),
        scratch_shapes=[pltpu.VMEM((r_loc, D_MODEL), ys.dtype), pltpu.SemaphoreType.DMA(())],
        compiler_params=_cparams(("arbitrary",)),
        name="combine",
    )(nseg, lss_flat, gseg, idx, rank, lss3, gates, x1, fscale, ys)


def _block_diag_groups(wa, wx):
    per = LRU_GROUP // LRU_BLOCK

    def bd(w):
        w = w.reshape(N_LRU_GROUPS, per, LRU_BLOCK, LRU_BLOCK)
        eye = jnp.eye(per, dtype=w.dtype)
        return jnp.einsum('gnij,nm->gnimj', w, eye).reshape(N_LRU_GROUPS, LRU_GROUP, LRU_GROUP)

    return jnp.concatenate([bd(wa), bd(wx)], axis=-1).astype(jnp.bfloat16)


def _pick_tile(n, pref):
    t = min(pref, n)
    while n % t:
        t //= 2
    return t


def _layer(x, norm1_scale, w_in, f_bias, gate_bias, conv_w, conv_b, rg_wa, rg_ba, rg_wx, rg_bx,
           rg_lambda, w_attn_proj, w_rec_proj, w_out, norm2_scale, router_w, router_b,
           moe_w1, moe_b1, moe_w2, moe_b2, final_scale):
    B, S, D = x.shape
    T = B * S
    A = T * TOP_K
    x2 = x.reshape(T, D)
    row = lambda v: v.reshape(1, -1).astype(jnp.float32)

    aw = N_HEADS * HEAD_DIM
    o_f = 3 * aw
    o_x = o_f + N_HEADS
    w_main = jnp.concatenate([w_in[:, :aw] * (HEAD_DIM ** -0.5 * LOG2E), w_in[:, aw:o_f], w_in[:, o_x:]],
                             axis=1).astype(jnp.bfloat16)
    w_f = jnp.pad(w_in[:, o_f:o_x], ((0, 0), (0, LANES - N_HEADS))).astype(jnp.bfloat16)
    f_b = jnp.pad(row(f_bias), ((0, 0), (0, LANES - N_HEADS)))

    proj, f_logit = _in_proj(x2, row(norm1_scale), w_main, w_f, _pick_tile(T, 1024))
    proj3 = proj.reshape(B, S, PROJ_COLS)

    c = _fgate(f_logit.reshape(B, S, LANES), f_b, chunk=_pick_tile(S, 256))
    attn = _attention(proj3, c.reshape(B, N_SLABS, HEADS_PER_SLAB, S), _pick_tile(S, 128),
                      _pick_tile(S, 256))

    rec = _rglru(proj3, conv_w.astype(jnp.float32), row(conv_b), _block_diag_groups(rg_wa, rg_wx),
                 row(rg_ba), row(rg_bx), row(rg_lambda), _pick_tile(S, 512))

    rw = jnp.pad(router_w.astype(jnp.float32), ((0, 0), (0, LANES - N_EXPERTS)))
    rb = jnp.pad(row(router_b), ((0, 0), (0, LANES - N_EXPERTS)), constant_values=NEG_BIG)
    gb = row(gate_bias)
    tm = _pick_tile(T, 512)
    x1, xn2, top_idx, gates, rank, tile_cnt = _merge(
        attn.reshape(T, D), rec.reshape(T, D), proj, x2,
        w_attn_proj.astype(jnp.bfloat16), w_rec_proj.astype(jnp.bfloat16), w_out.astype(jnp.bfloat16),
        gb[:, :D], gb[:, D:], row(norm2_scale), rw, rb, tm)

    n_tiles = T // tm
    te = 512
    n_blocks = -(-(A + (SEG - 1) * N_EXPERTS * n_tiles) // te) + N_EXPERTS
    n_rows = n_blocks * te
    r_loc = -(-(TOP_K * tm + (SEG - 1) * N_EXPERTS) // 256) * 256
    cnt = tile_cnt[:, 0, :N_EXPERTS]
    seg = (cnt + SEG - 1) // SEG * SEG
    lss = jnp.cumsum(seg, axis=1) - seg
    total = jnp.sum(seg, axis=0)
    padded = (total + te - 1) // te * te
    pad_end = jnp.cumsum(padded)
    pad_start = pad_end - padded
    gseg = pad_start[None, :] + jnp.cumsum(seg, axis=0) - seg
    n_used = (pad_end[-1] // te).astype(jnp.int32)
    blk = jnp.arange(n_blocks, dtype=jnp.int32)
    be = jnp.sum(pad_end[None, :] <= (blk * te)[:, None], axis=-1)
    block_expert = jnp.minimum(jnp.minimum(be, be[jnp.maximum(n_used - 1, 0)]),
                               N_EXPERTS - 1).astype(jnp.int32)
    fill = jnp.stack([jnp.concatenate([pad_start + total, pad_end[-1:]]),
                      jnp.concatenate([pad_end, jnp.full((1,), n_rows, pad_end.dtype)])]).astype(jnp.int32)
    flat = lambda a: a.reshape(-1).astype(jnp.int32)
    lss3 = jnp.pad(lss, ((0, 0), (0, LANES - N_EXPERTS))).reshape(n_tiles, 1, LANES).astype(jnp.int32)

    xs = _dispatch(flat(seg // SEG), flat(lss), flat(gseg), fill, xn2, top_idx, rank, lss3,
                   n_rows, tm, r_loc)
    ys = _moe(block_expert, n_used.reshape(1), xs, moe_w1, moe_b1.reshape(N_EXPERTS, 1, -1),
              moe_w2, moe_b2.reshape(N_EXPERTS, 1, -1), te)
    out = _combine(flat(seg // SEG), flat(lss), flat(gseg), top_idx, rank, lss3, gates, x1,
                   row(final_scale), ys, tm, r_loc)
    return out.reshape(B, S, D)


def kernel(x, norm1_scale, w_in, f_bias, gate_bias, conv_w, conv_b, rg_wa, rg_ba, rg_wx, rg_bx, rg_lambda, w_attn_proj, w_rec_proj, w_out, norm2_scale, router_w, router_b, moe_w1, moe_b1, moe_w2, moe_b2, final_norm_scale):
    depth = norm1_scale.shape[0]
    assert depth == 1, "the final rmsnorm is fused into the single layer's combine step"
    l = 0
    return _layer(x, norm1_scale[l], w_in[l], f_bias[l], gate_bias[l], conv_w[l], conv_b[l],
                  rg_wa[l], rg_ba[l], rg_wx[l], rg_bx[l], rg_lambda[l], w_attn_proj[l],
                  w_rec_proj[l], w_out[l], norm2_scale[l], router_w[l], router_b[l],
                  moe_w1[l], moe_b1[l], moe_w2[l], moe_b2[l], final_norm_scale)
```

```python
import functools

import jax
import jax.numpy as jnp
from jax import lax
from jax.experimental import pallas as pl
from jax.experimental.pallas import tpu as pltpu

D_MODEL = 1024
N_HEADS = 16
HEAD_DIM = 64
LANES = 128
HEADS_PER_SLAB = LANES // HEAD_DIM
N_SLABS = D_MODEL // LANES
N_LRU_BLOCKS = 16
LRU_BLOCK = 64
LRU_GROUP = 256
N_LRU_GROUPS = D_MODEL // LRU_GROUP
CONV_W = 4
LRU_C = 8.0
N_EXPERTS = 32
TOP_K = 4
SWIGLU_ALPHA = 1.702
SWIGLU_LIMIT = 7.0
RMS_EPS = 1e-6
NEG_BIG = -1e30
LOG2E = 1.4426950408889634

PROJ_COLS = 7 * D_MODEL
COL_Q, COL_K, COL_V, COL_XREC, COL_YREC, COL_GA, COL_GR = range(7)

VMEM_LIMIT = 56 * 1024 * 1024


def _cparams(sem):
    return pltpu.CompilerParams(dimension_semantics=sem, vmem_limit_bytes=VMEM_LIMIT)


def _softplus(z):
    return jnp.maximum(z, 0.0) + jnp.log1p(jnp.exp(-jnp.abs(z)))


def _sigmoid(z):
    return 0.5 * jnp.tanh(0.5 * z) + 0.5


def _rms(x, scale):
    return x * lax.rsqrt(jnp.mean(x * x, axis=-1, keepdims=True) + RMS_EPS) * scale


def _in_proj_kernel(x_ref, scale_ref, w_ref, wf_ref, proj_ref, f_ref, xn_ref):
    @pl.when(pl.program_id(1) == 0)
    def _():
        xn = _rms(x_ref[...], scale_ref[...]).astype(jnp.bfloat16)
        xn_ref[...] = xn
        f_ref[...] = jnp.dot(xn, wf_ref[...], preferred_element_type=jnp.float32)

    proj_ref[...] = jnp.dot(xn_ref[...], w_ref[...],
                            preferred_element_type=jnp.float32).astype(proj_ref.dtype)


def _in_proj(x2, scale, w, wf, tm):
    T = x2.shape[0]
    n_col = w.shape[1] // D_MODEL
    return pl.pallas_call(
        _in_proj_kernel,
        grid=(T // tm, n_col),
        in_specs=[
            pl.BlockSpec((tm, D_MODEL), lambda i, j: (i, 0)),
            pl.BlockSpec((1, D_MODEL), lambda i, j: (0, 0)),
            pl.BlockSpec((D_MODEL, D_MODEL), lambda i, j: (0, j)),
            pl.BlockSpec((D_MODEL, LANES), lambda i, j: (0, 0)),
        ],
        out_specs=[
            pl.BlockSpec((tm, D_MODEL), lambda i, j: (i, j)),
            pl.BlockSpec((tm, LANES), lambda i, j: (i, 0)),
        ],
        out_shape=[
            jax.ShapeDtypeStruct((T, w.shape[1]), jnp.bfloat16),
            jax.ShapeDtypeStruct((T, LANES), jnp.float32),
        ],
        scratch_shapes=[pltpu.VMEM((tm, D_MODEL), jnp.bfloat16)],
        compiler_params=_cparams(("parallel", "arbitrary")),
        name="in_proj",
    )(x2, scale, w, wf)


def _fgate_kernel(f_ref, fb_ref, c_ref, *, chunk):
    S = f_ref.shape[0]
    row = lax.broadcasted_iota(jnp.int32, (chunk, chunk), 0)
    col = lax.broadcasted_iota(jnp.int32, (chunk, chunk), 1)
    tri = (row >= col).astype(jnp.float32)
    carry = jnp.zeros((1, LANES), jnp.float32)
    for c in range(S // chunk):
        z = f_ref[pl.ds(c * chunk, chunk), :] + fb_ref[...]
        log_f = -_softplus(-z)
        cs = jnp.dot(tri, log_f, precision=lax.Precision.HIGHEST,
                     preferred_element_type=jnp.float32) + carry
        carry = cs[chunk - 1:chunk, :]
        c_ref[:, pl.ds(c * chunk, chunk)] = (cs * LOG2E).T[:N_HEADS, :]


def _fgate(f3, fb, chunk=256):
    B, S, _ = f3.shape
    return pl.pallas_call(
        functools.partial(_fgate_kernel, chunk=chunk),
        grid=(B,),
        in_specs=[
            pl.BlockSpec((None, S, LANES), lambda b: (b, 0, 0)),
            pl.BlockSpec((1, LANES), lambda b: (0, 0)),
        ],
        out_specs=pl.BlockSpec((None, N_HEADS, S), lambda b: (b, 0, 0)),
        out_shape=jax.ShapeDtypeStruct((B, N_HEADS, S), jnp.float32),
        compiler_params=_cparams(("parallel",)),
        name="fgate",
    )(f3, fb)


def _attn_kernel(q_ref, k_ref, v_ref, c_ref, o_ref, *, tq, tk):
    S = q_ref.shape[0]
    lane = lax.broadcasted_iota(jnp.int32, (1, LANES), 1)
    head0 = lane < HEAD_DIM
    row = lax.broadcasted_iota(jnp.int32, (HEADS_PER_SLAB * tq, tk), 0) % tq
    col = lax.broadcasted_iota(jnp.int32, (HEADS_PER_SLAB * tq, tk), 1)

    def chunk(ks, carry, q2, mask):
        m, l, acc = carry
        s = lax.dot_general(q2, k_ref[pl.ds(ks, tk), :], (((1,), (1,)), ((), ())),
                            preferred_element_type=jnp.float32)
        cb = c_ref[:, pl.ds(ks, tk)]
        s = jnp.concatenate([s[:tq] - cb[0:1], s[tq:] - cb[1:2]], axis=0)
        if mask is not None:
            s = jnp.where(mask, s, NEG_BIG)
        m_new = jnp.maximum(m, jnp.max(s, axis=-1, keepdims=True))
        alpha = jnp.exp2(m - m_new)
        p = jnp.exp2(s - m_new)
        l = alpha * l + jnp.sum(p, axis=-1, keepdims=True)
        pv = jnp.dot(p.astype(jnp.bfloat16), v_ref[pl.ds(ks, tk), :],
                     preferred_element_type=jnp.float32)
        return m_new, l, alpha * acc + pv

    for qi in range(S // tq):
        qs = qi * tq
        q = q_ref[pl.ds(qs, tq), :]
        zero = jnp.zeros_like(q)
        q2 = jnp.concatenate([jnp.where(head0, q, zero), jnp.where(head0, zero, q)], axis=0)
        carry = (jnp.full((HEADS_PER_SLAB * tq, 1), NEG_BIG, jnp.float32),
                 jnp.zeros((HEADS_PER_SLAB * tq, 1), jnp.float32),
                 jnp.zeros((HEADS_PER_SLAB * tq, LANES), jnp.float32))
        n_full = qs // tk
        n_chunks = -(-(qs + tq) // tk)
        for kc in range(n_chunks):
            mask = None if kc < n_full else (row + qs) >= (col + kc * tk)
            carry = chunk(kc * tk, carry, q2, mask)
        _, l, acc = carry
        out = acc / l
        o_ref[pl.ds(qs, tq), :] = jnp.where(head0, out[:tq], out[tq:]).astype(o_ref.dtype)


def _attention(proj3, c4, tq, tk):
    B, S, _ = proj3.shape
    slab = lambda c: pl.BlockSpec((None, S, LANES), lambda b, p: (b, 0, c * N_SLABS + p))
    return pl.pallas_call(
        functools.partial(_attn_kernel, tq=tq, tk=tk),
        grid=(B, N_SLABS),
        in_specs=[slab(COL_Q), slab(COL_K), slab(COL_V),
                  pl.BlockSpec((None, None, HEADS_PER_SLAB, S), lambda b, p: (b, p, 0, 0))],
        out_specs=pl.BlockSpec((None, S, LANES), lambda b, p: (b, 0, p)),
        out_shape=jax.ShapeDtypeStruct((B, S, D_MODEL), jnp.bfloat16),
        compiler_params=_cparams(("parallel", "parallel")),
        name="attn",
    )(proj3, proj3, proj3, c4)


def _rglru_kernel(x_ref, y_ref, cw_ref, cb_ref, wg_ref, ba_ref, bx_ref, lam_ref, o_ref,
                  xe_ref, a_ref, b_ref, h_ref, *, ts):
    HALO = 8
    s_idx = pl.program_id(1)

    @pl.when(s_idx == 0)
    def _():
        xe_ref[pl.ds(0, HALO), :] = jnp.zeros((HALO, D_MODEL), jnp.float32)
        h_ref[...] = jnp.zeros_like(h_ref)

    xe_ref[pl.ds(HALO, ts), :] = x_ref[...].astype(jnp.float32)
    conv = cb_ref[...] + cw_ref[CONV_W - 1:CONV_W, :] * xe_ref[pl.ds(HALO, ts), :]
    for j in range(CONV_W - 1):
        shift = CONV_W - 1 - j
        conv = conv + cw_ref[j:j + 1, :] * xe_ref[pl.ds(HALO - shift, ts), :]
    xe_ref[pl.ds(0, HALO), :] = xe_ref[pl.ds(ts, HALO), :]

    cb16 = conv.astype(jnp.bfloat16)
    sp = _softplus(-lam_ref[...])
    first = (lax.broadcasted_iota(jnp.int32, (ts, 1), 0) + s_idx * ts) == 0
    for g in range(N_LRU_GROUPS):
        cols = slice(g * LRU_GROUP, (g + 1) * LRU_GROUP)
        gates = jnp.dot(cb16[:, cols], wg_ref[g], preferred_element_type=jnp.float32)
        r = _sigmoid(gates[:, :LRU_GROUP] + ba_ref[:, cols])
        i = _sigmoid(gates[:, LRU_GROUP:] + bx_ref[:, cols])
        log_a = -LRU_C * r * sp[:, cols]
        a = jnp.exp(log_a)
        mult = jnp.where(first, 1.0, jnp.sqrt(1.0 - a * a))
        a_ref[:, cols] = a
        b_ref[:, cols] = mult * (i * conv[:, cols])

    SUB = 8
    rows = lax.broadcasted_iota(jnp.int32, (SUB, D_MODEL), 0)

    def slab(i, h_prev):
        start = pl.multiple_of(i * SUB, SUB)
        a = a_ref[pl.ds(start, SUB), :]
        b = b_ref[pl.ds(start, SUB), :]
        for d in (1, 2, 4):
            keep = rows >= d
            a_s = jnp.where(keep, pltpu.roll(a, d, 0), 1.0)
            b_s = jnp.where(keep, pltpu.roll(b, d, 0), 0.0)
            b = a * b_s + b
            a = a * a_s
        h = a * h_prev + b
        gate = jax.nn.gelu(y_ref[pl.ds(start, SUB), :].astype(jnp.float32))
        o_ref[pl.ds(start, SUB), :] = (h * gate).astype(o_ref.dtype)
        return jnp.broadcast_to(h[SUB - 1:SUB, :], (SUB, D_MODEL))

    h_ref[...] = lax.fori_loop(0, ts // SUB, slab, h_ref[...], unroll=4)


def _rglru(proj3, conv_w, conv_b, wg, ba, bx, lam, ts):
    B, S, _ = proj3.shape
    full = lambda shape: pl.BlockSpec(shape, lambda b, s: (0,) * len(shape))
    return pl.pallas_call(
        functools.partial(_rglru_kernel, ts=ts),
        grid=(B, S // ts),
        in_specs=[
            pl.BlockSpec((None, ts, D_MODEL), lambda b, s: (b, s, COL_XREC)),
            pl.BlockSpec((None, ts, D_MODEL), lambda b, s: (b, s, COL_YREC)),
            full((CONV_W, D_MODEL)),
            full((1, D_MODEL)),
            full((N_LRU_GROUPS, LRU_GROUP, 2 * LRU_GROUP)),
            full((1, D_MODEL)),
            full((1, D_MODEL)),
            full((1, D_MODEL)),
        ],
        out_specs=pl.BlockSpec((None, ts, D_MODEL), lambda b, s: (b, s, 0)),
        out_shape=jax.ShapeDtypeStruct((B, S, D_MODEL), jnp.bfloat16),
        scratch_shapes=[
            pltpu.VMEM((ts + 8, D_MODEL), jnp.float32),
            pltpu.VMEM((ts, D_MODEL), jnp.float32),
            pltpu.VMEM((ts, D_MODEL), jnp.float32),
            pltpu.VMEM((8, D_MODEL), jnp.float32),
        ],
        compiler_params=_cparams(("parallel", "arbitrary")),
        name="rglru",
    )(proj3, proj3, conv_w, conv_b, wg, ba, bx, lam)


def _merge_kernel(attn_ref, rec_ref, ga_ref, gr_ref, x_ref, wa_ref, wr_ref, wo_ref,
                  gba_ref, gbr_ref, n2_ref, rw_ref, rb_ref,
                  x1_ref, xn2_ref, idx_ref, gate_ref, rank_ref, cnt_ref, *, tm):
    pa = jnp.dot(attn_ref[...], wa_ref[...], preferred_element_type=jnp.float32)
    pr = jnp.dot(rec_ref[...], wr_ref[...], preferred_element_type=jnp.float32)
    g_a = _sigmoid(ga_ref[...].astype(jnp.float32) + gba_ref[...])
    g_r = _sigmoid(gr_ref[...].astype(jnp.float32) + gbr_ref[...])
    merged = (g_a * pa + g_r * pr).astype(jnp.bfloat16)
    x1 = x_ref[...] + jnp.dot(merged, wo_ref[...], preferred_element_type=jnp.float32)
    x1_ref[...] = x1
    xn2 = _rms(x1, n2_ref[...])
    xh = xn2.astype(jnp.bfloat16)
    xn2_ref[...] = xh

    xl = (xn2 - xh.astype(jnp.float32)).astype(jnp.bfloat16)
    part = jnp.dot(xh, rw_ref[...], preferred_element_type=jnp.float32)
    logits = (part[:, :LANES] + part[:, LANES:]
              + jnp.dot(xl, rw_ref[:, :LANES], preferred_element_type=jnp.float32)) + rb_ref[...]
    lane = lax.broadcasted_iota(jnp.int32, (tm, LANES), 1).astype(jnp.float32)
    idx_out = jnp.zeros((tm, LANES), jnp.float32)
    val_out = jnp.zeros((tm, LANES), jnp.float32)
    onehot = jnp.zeros((tm, LANES), jnp.float32)
    sel = []
    work = logits
    for j in range(TOP_K):
        m = jnp.max(work, axis=-1, keepdims=True)
        idx = jnp.min(jnp.where(work == m, lane, float(LANES)), axis=-1, keepdims=True)
        hit = lane == idx
        sel.append(hit)
        onehot = jnp.where(hit, 1.0, onehot)
        idx_out = jnp.where(lane == float(j), idx, idx_out)
        val_out = jnp.where(lane == float(j), m, val_out)
        work = jnp.where(hit, NEG_BIG, work)
    is_slot = lane < float(TOP_K)
    e = jnp.where(is_slot, jnp.exp(val_out - jnp.max(jnp.where(is_slot, val_out, NEG_BIG),
                                                      axis=-1, keepdims=True)), 0.0)
    gate_ref[...] = e / jnp.sum(e, axis=-1, keepdims=True)
    idx_ref[...] = idx_out.astype(jnp.int32)

    row = lax.broadcasted_iota(jnp.int32, (tm, tm), 0)
    col = lax.broadcasted_iota(jnp.int32, (tm, tm), 1)
    strict = (row > col).astype(jnp.bfloat16)
    before = jnp.dot(strict, onehot.astype(jnp.bfloat16), preferred_element_type=jnp.float32)
    rank_out = jnp.zeros((tm, LANES), jnp.float32)
    for j in range(TOP_K):
        rj = jnp.sum(jnp.where(sel[j], before, 0.0), axis=-1, keepdims=True)
        rank_out = jnp.where(lane == float(j), rj, rank_out)
    rank_ref[...] = rank_out.astype(jnp.int32)
    cnt_ref[...] = jnp.sum(onehot, axis=0, keepdims=True).astype(jnp.int32)


def _merge(attn2, rec2, proj2, x2, wa, wr, wo, gba, gbr, n2, rw, rb, tm):
    T = x2.shape[0]
    tok = lambda c: pl.BlockSpec((tm, D_MODEL), lambda i: (i, c))
    full = lambda shape: pl.BlockSpec(shape, lambda i: (0,) * len(shape))
    meta = pl.BlockSpec((tm, LANES), lambda i: (i, 0))
    return pl.pallas_call(
        functools.partial(_merge_kernel, tm=tm),
        grid=(T // tm,),
        in_specs=[tok(0), tok(0), tok(COL_GA), tok(COL_GR), tok(0),
                  full((D_MODEL, D_MODEL)), full((D_MODEL, D_MODEL)), full((D_MODEL, D_MODEL)),
                  full((1, D_MODEL)), full((1, D_MODEL)), full((1, D_MODEL)),
                  full((D_MODEL, 2 * LANES)), full((1, LANES))],
        out_specs=[tok(0), tok(0), meta, meta, meta,
                   pl.BlockSpec((None, 1, LANES), lambda i: (i, 0, 0))],
        out_shape=[
            jax.ShapeDtypeStruct((T, D_MODEL), jnp.float32),
            jax.ShapeDtypeStruct((T, D_MODEL), jnp.bfloat16),
            jax.ShapeDtypeStruct((T, LANES), jnp.int32),
            jax.ShapeDtypeStruct((T, LANES), jnp.float32),
            jax.ShapeDtypeStruct((T, LANES), jnp.int32),
            jax.ShapeDtypeStruct((T // tm, 1, LANES), jnp.int32),
        ],
        compiler_params=_cparams(("parallel",)),
        name="merge_router",
    )(attn2, rec2, proj2, proj2, x2, wa, wr, wo, gba, gbr, n2, rw, rb)


SEG = 16
SEG_PIECES = 6
FILL_ROWS = SEG << (SEG_PIECES - 1)


def _segment_copies(src, src_off, dst, dst_off, n_seg, sem, wait):
    for k in range(SEG_PIECES):
        @pl.when(((n_seg >> k) & 1) == 1)
        def _(k=k):
            off = (n_seg & ((1 << k) - 1)) * SEG
            cp = pltpu.make_async_copy(
                src.at[pl.ds(pl.multiple_of(src_off + off, SEG), SEG << k), :],
                dst.at[pl.ds(pl.multiple_of(dst_off + off, SEG), SEG << k), :], sem)
            if wait:
                cp.wait()
            else:
                cp.start()


TABLE_LEAD = 2
TABLE_TAIL = 1


def _tile_copies(local_fn, hbm_fn, to_hbm, row, nseg_ref, lss_ref, gseg_ref, sem, wait):
    def per_expert(e, carry):
        k = row * N_EXPERTS + e
        local, hbm = (local_fn(), lss_ref[k]), (hbm_fn(), gseg_ref[k])
        (src, src_off), (dst, dst_off) = (local, hbm) if to_hbm else (hbm, local)
        _segment_copies(src, src_off, dst, dst_off, nseg_ref[k], sem, wait)
        return carry

    lax.fori_loop(0, N_EXPERTS, per_expert, 0)


def _local_positions(idx_ref, rank_ref, lss_ref):
    tm = idx_ref.shape[0]
    lane = lax.broadcasted_iota(jnp.int32, (tm, LANES), 1)
    idx = idx_ref[...]
    lss = lss_ref[...].astype(jnp.float32)
    pos = jnp.zeros((tm, LANES), jnp.float32)
    for j in range(TOP_K):
        start = jnp.sum(jnp.where(lane == idx[:, j:j + 1], lss, 0.0), axis=-1, keepdims=True)
        pos = jnp.where(lane == j, start, pos)
    return pos + rank_ref[...].astype(jnp.float32)


def _dispatch_kernel(nseg_ref, lss_s_ref, gseg_ref, fill_ref, xn_ref, idx_ref, rank_ref, lss_ref,
                     zero_ref, xs_ref, buf_ref, sem, *, rc):
    i = pl.program_id(0)
    tm = xn_ref.shape[0]
    r_loc = buf_ref.shape[1]

    @pl.when(i == 0)
    def _():
        fill_sem = sem.at[0]

        def fill_range(e, carry):
            lo = fill_ref[0, e]
            n = fill_ref[1, e] - lo
            n_big = n // FILL_ROWS
            rem = (n - n_big * FILL_ROWS) // SEG

            def big(b, wait):
                cp = pltpu.make_async_copy(
                    zero_ref, xs_ref.at[pl.ds(pl.multiple_of(lo + b * FILL_ROWS, SEG), FILL_ROWS), :],
                    fill_sem)
                if wait:
                    cp.wait()
                else:
                    cp.start()

            lax.fori_loop(0, n_big, lambda b, c: (big(b, False), c)[1], 0)
            _segment_copies(zero_ref, 0, xs_ref, lo + n_big * FILL_ROWS, rem, fill_sem, False)
            lax.fori_loop(0, n_big, lambda b, c: (big(b, True), c)[1], 0)
            _segment_copies(zero_ref, 0, xs_ref, lo + n_big * FILL_ROWS, rem, fill_sem, True)
            return carry

        lax.fori_loop(0, N_EXPERTS + 1, fill_range, 0)

    slot = i % 2
    _tile_copies(lambda: buf_ref.at[slot], lambda: xs_ref, True, i, nseg_ref, lss_s_ref, gseg_ref,
                 sem.at[slot], wait=True)

    pos_t = _local_positions(idx_ref, rank_ref, lss_ref).T[:TOP_K, :]
    x = xn_ref[...]
    for c in range(r_loc // rc):
        rho = (lax.broadcasted_iota(jnp.int32, (rc, tm), 0) + c * rc).astype(jnp.float32)
        hit = rho == pos_t[0:1, :]
        for j in range(1, TOP_K):
            hit = hit | (rho == pos_t[j:j + 1, :])
        perm = jnp.where(hit, 1.0, 0.0).astype(jnp.bfloat16)
        buf_ref[slot, pl.ds(c * rc, rc), :] = jnp.dot(
            perm, x, preferred_element_type=jnp.float32).astype(buf_ref.dtype)

    _tile_copies(lambda: buf_ref.at[1 - slot], lambda: xs_ref, True, i + 1, nseg_ref, lss_s_ref,
                 gseg_ref, sem.at[1 - slot], wait=False)

    @pl.when(i == pl.num_programs(0) - 1)
    def _():
        _tile_copies(lambda: buf_ref.at[1 - slot], lambda: xs_ref, True, i + 1, nseg_ref, lss_s_ref,
                     gseg_ref, sem.at[1 - slot], wait=True)


def _dispatch(nseg, lss_flat, gseg, fill, xn2, idx, rank, lss3, n_rows, tm, r_loc):
    n_tiles = xn2.shape[0] // tm
    zeros = jnp.zeros((FILL_ROWS, D_MODEL), xn2.dtype)
    smem = pl.BlockSpec(memory_space=pltpu.SMEM)
    tile = lambda i: jnp.minimum(i, n_tiles - 1)
    meta = pl.BlockSpec((tm, LANES), lambda i: (tile(i), 0))
    return pl.pallas_call(
        functools.partial(_dispatch_kernel, rc=256),
        grid=(n_tiles + 1,),
        in_specs=[smem, smem, smem, smem,
                  pl.BlockSpec((tm, D_MODEL), lambda i: (tile(i), 0)), meta, meta,
                  pl.BlockSpec((None, 1, LANES), lambda i: (tile(i), 0, 0)),
                  pl.BlockSpec((FILL_ROWS, D_MODEL), lambda i: (0, 0))],
        out_specs=pl.BlockSpec(memory_space=pl.ANY),
        out_shape=jax.ShapeDtypeStruct((n_rows, D_MODEL), xn2.dtype),
        scratch_shapes=[pltpu.VMEM((2, r_loc, D_MODEL), xn2.dtype), pltpu.SemaphoreType.DMA((2,))],
        compiler_params=_cparams(("arbitrary",)),
        name="dispatch",
    )(nseg, lss_flat, gseg, fill, xn2, idx, rank, lss3, zeros)


def _moe_kernel(be_ref, nu_ref, xs_ref, w1_ref, b1_ref, w2_ref, b2_ref, ys_ref, w1b_ref, w2b_ref):
    i = pl.program_id(0)
    prev = be_ref[jnp.maximum(i - 1, 0)]

    @pl.when((i == 0) | (be_ref[i] != prev))
    def _():
        w1b_ref[...] = w1_ref[...].astype(jnp.bfloat16)
        w2b_ref[...] = w2_ref[...].astype(jnp.bfloat16)

    @pl.when(i < nu_ref[0])
    def _():
        h = jnp.dot(xs_ref[...], w1b_ref[...],
                    preferred_element_type=jnp.float32) + b1_ref[...]
        d = h.shape[1] // 2
        glu = jnp.minimum(h[:, :d], SWIGLU_LIMIT)
        lin = jnp.clip(h[:, d:], -SWIGLU_LIMIT, SWIGLU_LIMIT)
        act = glu * _sigmoid(SWIGLU_ALPHA * glu) * (lin + 1.0)
        y = jnp.dot(act.astype(jnp.bfloat16), w2b_ref[...],
                    preferred_element_type=jnp.float32) + b2_ref[...]
        ys_ref[...] = y.astype(ys_ref.dtype)

    @pl.when(i >= nu_ref[0])
    def _():
        ys_ref[...] = jnp.zeros_like(ys_ref)


def _moe(block_expert, n_used, xs, w1, b1, w2, b2, te):
    n_blocks = xs.shape[0] // te
    d_ff2 = w1.shape[2]
    grid_spec = pltpu.PrefetchScalarGridSpec(
        num_scalar_prefetch=2,
        grid=(n_blocks,),
        in_specs=[
            pl.BlockSpec((te, D_MODEL), lambda i, be, nu: (jnp.clip(i, 0, jnp.maximum(nu[0] - 1, 0)), 0)),
            pl.BlockSpec((None, D_MODEL, d_ff2), lambda i, be, nu: (be[i], 0, 0)),
            pl.BlockSpec((None, 1, d_ff2), lambda i, be, nu: (be[i], 0, 0)),
            pl.BlockSpec((None, d_ff2 // 2, D_MODEL), lambda i, be, nu: (be[i], 0, 0)),
            pl.BlockSpec((None, 1, D_MODEL), lambda i, be, nu: (be[i], 0, 0)),
        ],
        out_specs=pl.BlockSpec((te, D_MODEL), lambda i, be, nu: (i, 0)),
        scratch_shapes=[pltpu.VMEM((D_MODEL, d_ff2), jnp.bfloat16),
                        pltpu.VMEM((d_ff2 // 2, D_MODEL), jnp.bfloat16)],
    )
    return pl.pallas_call(
        _moe_kernel,
        grid_spec=grid_spec,
        out_shape=jax.ShapeDtypeStruct(xs.shape, xs.dtype),
        compiler_params=_cparams(("arbitrary",)),
        name="moe_experts",
    )(block_expert, n_used, xs, w1, b1, w2, b2)


def _combine_kernel(nseg_ref, lss_s_ref, gseg_ref, idx_ref, rank_ref, lss_ref, gate_ref, x1_ref,
                    fs_ref, ys_ref, o_ref, buf_ref, sem, *, rc):
    i = pl.program_id(0)
    tm = x1_ref.shape[0]
    r_loc = buf_ref.shape[1]
    slot = i % 2
    row = i + TABLE_LEAD

    @pl.when(i == 0)
    def _():
        buf_ref[...] = jnp.zeros_like(buf_ref)
        _tile_copies(lambda: buf_ref.at[0], lambda: ys_ref, False, TABLE_LEAD, nseg_ref, lss_s_ref,
                     gseg_ref, sem.at[0], wait=False)

    _tile_copies(lambda: buf_ref.at[1 - slot], lambda: ys_ref, False, row + 1, nseg_ref, lss_s_ref,
                 gseg_ref, sem.at[1 - slot], wait=False)
    _tile_copies(lambda: buf_ref.at[slot], lambda: ys_ref, False, row, nseg_ref, lss_s_ref,
                 gseg_ref, sem.at[slot], wait=True)

    pos = _local_positions(idx_ref, rank_ref, lss_ref)
    gates = gate_ref[...]
    acc = x1_ref[...]
    for c in range(r_loc // rc):
        rho = (lax.broadcasted_iota(jnp.int32, (tm, rc), 1) + c * rc).astype(jnp.float32)
        g = jnp.zeros((tm, rc), jnp.float32)
        for j in range(TOP_K):
            g = g + jnp.where(rho == pos[:, j:j + 1], gates[:, j:j + 1], 0.0)
        acc = acc + jnp.dot(g.astype(jnp.bfloat16), buf_ref[slot, pl.ds(c * rc, rc), :],
                            preferred_element_type=jnp.float32)
    o_ref[...] = _rms(acc, fs_ref[...])


def _combine(nseg, lss_flat, gseg, idx, rank, lss3, gates, x1, fscale, ys, tm, r_loc):
    T = x1.shape[0]
    smem = pl.BlockSpec(memory_space=pltpu.SMEM)
    meta = pl.BlockSpec((tm, LANES), lambda i: (i, 0))
    return pl.pallas_call(
        functools.partial(_combine_kernel, rc=256),
        grid=(T // tm,),
        in_specs=[smem, smem, smem, meta, meta,
                  pl.BlockSpec((None, 1, LANES), lambda i: (i, 0, 0)), meta,
                  pl.BlockSpec((tm, D_MODEL), lambda i: (i, 0)),
                  pl.BlockSpec((1, D_MODEL), lambda i: (0, 0)),
                  pl.BlockSpec(memory_space=pl.ANY)],
        out_specs=pl.BlockSpec((tm, D_MODEL), lambda i: (i, 0)),
        out_shape=jax.ShapeDtypeStruct((T, D_MODEL), jnp.float32),
        scratch_shapes=[pltpu.VMEM((2, r_loc, D_MODEL), ys.dtype), pltpu.SemaphoreType.DMA((2,))],
        compiler_params=_cparams(("arbitrary",)),
        name="combine",
    )(nseg, lss_flat, gseg, idx, rank, lss3, gates, x1, fscale, ys)


def _block_diag_groups(wa, wx):
    per = LRU_GROUP // LRU_BLOCK

    def bd(w):
        w = w.reshape(N_LRU_GROUPS, per, LRU_BLOCK, LRU_BLOCK)
        eye = jnp.eye(per, dtype=w.dtype)
        return jnp.einsum('gnij,nm->gnimj', w, eye).reshape(N_LRU_GROUPS, LRU_GROUP, LRU_GROUP)

    return jnp.concatenate([bd(wa), bd(wx)], axis=-1).astype(jnp.bfloat16)


def _pick_tile(n, pref):
    t = min(pref, n)
    while n % t:
        t //= 2
    return t


def _layer(x, norm1_scale, w_in, f_bias, gate_bias, conv_w, conv_b, rg_wa, rg_ba, rg_wx, rg_bx,
           rg_lambda, w_attn_proj, w_rec_proj, w_out, norm2_scale, router_w, router_b,
           moe_w1, moe_b1, moe_w2, moe_b2, final_scale):
    B, S, D = x.shape
    T = B * S
    A = T * TOP_K
    x2 = x.reshape(T, D)
    row = lambda v: v.reshape(1, -1).astype(jnp.float32)

    aw = N_HEADS * HEAD_DIM
    o_f = 3 * aw
    o_x = o_f + N_HEADS
    w_main = jnp.concatenate([w_in[:, :aw] * (HEAD_DIM ** -0.5 * LOG2E), w_in[:, aw:o_f], w_in[:, o_x:]],
                             axis=1).astype(jnp.bfloat16)
    w_f = jnp.pad(w_in[:, o_f:o_x], ((0, 0), (0, LANES - N_HEADS))).astype(jnp.bfloat16)
    f_b = jnp.pad(row(f_bias), ((0, 0), (0, LANES - N_HEADS)))

    proj, f_logit = _in_proj(x2, row(norm1_scale), w_main, w_f, _pick_tile(T, 1024))
    proj3 = proj.reshape(B, S, PROJ_COLS)

    c = _fgate(f_logit.reshape(B, S, LANES), f_b, chunk=_pick_tile(S, 256))
    attn = _attention(proj3, c.reshape(B, N_SLABS, HEADS_PER_SLAB, S), _pick_tile(S, 256),
                      _pick_tile(S, 256))

    rec = _rglru(proj3, conv_w.astype(jnp.float32), row(conv_b), _block_diag_groups(rg_wa, rg_wx),
                 row(rg_ba), row(rg_bx), row(rg_lambda), _pick_tile(S, 512))

    rw32 = jnp.pad(router_w.astype(jnp.float32), ((0, 0), (0, LANES - N_EXPERTS)))
    rw_hi = rw32.astype(jnp.bfloat16)
    rw = jnp.concatenate([rw_hi, (rw32 - rw_hi.astype(jnp.float32)).astype(jnp.bfloat16)], axis=1)
    rb = jnp.pad(row(router_b), ((0, 0), (0, LANES - N_EXPERTS)), constant_values=NEG_BIG)
    gb = row(gate_bias)
    tm = _pick_tile(T, 512)
    x1, xn2, top_idx, gates, rank, tile_cnt = _merge(
        attn.reshape(T, D), rec.reshape(T, D), proj, x2,
        w_attn_proj.astype(jnp.bfloat16), w_rec_proj.astype(jnp.bfloat16), w_out.astype(jnp.bfloat16),
        gb[:, :D], gb[:, D:], row(norm2_scale), rw, rb, tm)

    n_tiles = T // tm
    te = 512
    n_blocks = -(-(A + (SEG - 1) * N_EXPERTS * n_tiles) // te) + N_EXPERTS
    n_rows = n_blocks * te
    r_loc = -(-(TOP_K * tm + (SEG - 1) * N_EXPERTS) // 256) * 256
    cnt = tile_cnt[:, 0, :N_EXPERTS]
    seg = (cnt + SEG - 1) // SEG * SEG
    lss = jnp.cumsum(seg, axis=1) - seg
    total = jnp.sum(seg, axis=0)
    padded = (total + te - 1) // te * te
    pad_end = jnp.cumsum(padded)
    pad_start = pad_end - padded
    gseg = pad_start[None, :] + jnp.cumsum(seg, axis=0) - seg
    n_used = (pad_end[-1] // te).astype(jnp.int32)
    blk = jnp.arange(n_blocks, dtype=jnp.int32)
    be = jnp.sum(pad_end[None, :] <= (blk * te)[:, None], axis=-1)
    block_expert = jnp.minimum(jnp.minimum(be, be[jnp.maximum(n_used - 1, 0)]),
                               N_EXPERTS - 1).astype(jnp.int32)
    fill = jnp.stack([jnp.concatenate([pad_start + total, pad_end[-1:]]),
                      jnp.concatenate([pad_end, jnp.full((1,), n_rows, pad_end.dtype)])]).astype(jnp.int32)
    flat = lambda a: jnp.pad(a, ((TABLE_LEAD, TABLE_TAIL), (0, 0))).reshape(-1).astype(jnp.int32)
    lss3 = jnp.pad(lss, ((0, 0), (0, LANES - N_EXPERTS))).reshape(n_tiles, 1, LANES).astype(jnp.int32)

    xs = _dispatch(flat(seg // SEG), flat(lss), flat(gseg), fill, xn2, top_idx, rank, lss3,
                   n_rows, tm, r_loc)
    ys = _moe(block_expert, n_used.reshape(1), xs, moe_w1, moe_b1.reshape(N_EXPERTS, 1, -1),
              moe_w2, moe_b2.reshape(N_EXPERTS, 1, -1), te)
    out = _combine(flat(seg // SEG), flat(lss), flat(gseg), top_idx, rank, lss3, gates, x1,
                   row(final_scale), ys, tm, r_loc)
    return out.reshape(B, S, D)


def kernel(x, norm1_scale, w_in, f_bias, gate_bias, conv_w, conv_b, rg_wa, rg_ba, rg_wx, rg_bx, rg_lambda, w_attn_proj, w_rec_proj, w_out, norm2_scale, router_w, router_b, moe_w1, moe_b1, moe_w2, moe_b2, final_norm_scale):
    depth = norm1_scale.shape[0]
    assert depth == 1, "the final rmsnorm is fused into the single layer's combine step"
    l = 0
    return _layer(x, norm1_scale[l], w_in[l], f_bias[l], gate_bias[l], conv_w[l], conv_b[l],
                  rg_wa[l], rg_ba[l], rg_wx[l], rg_bx[l], rg_lambda[l], w_attn_proj[l],
                  w_rec_proj[l], w_out[l], norm2_scale[l], router_w[l], router_b[l],
                  moe_w1[l], moe_b1[l], moe_w2[l], moe_b2[l], final_norm_scale)
```

```python
import functools

import jax
import jax.numpy as jnp
from jax import lax
from jax.experimental import pallas as pl
from jax.experimental.pallas import tpu as pltpu

D_MODEL = 1024
N_HEADS = 16
HEAD_DIM = 64
LANES = 128
HEADS_PER_SLAB = LANES // HEAD_DIM
N_SLABS = D_MODEL // LANES
N_LRU_BLOCKS = 16
LRU_BLOCK = 64
LRU_GROUP = 256
N_LRU_GROUPS = D_MODEL // LRU_GROUP
CONV_W = 4
LRU_C = 8.0
N_EXPERTS = 32
TOP_K = 4
SWIGLU_ALPHA = 1.702
SWIGLU_LIMIT = 7.0
RMS_EPS = 1e-6
NEG_BIG = -1e30
LOG2E = 1.4426950408889634

PROJ_COLS = 7 * D_MODEL
COL_Q, COL_K, COL_V, COL_XREC, COL_YREC, COL_GA, COL_GR = range(7)

VMEM_LIMIT = 56 * 1024 * 1024


def _cparams(sem):
    return pltpu.CompilerParams(dimension_semantics=sem, vmem_limit_bytes=VMEM_LIMIT)


def _softplus(z):
    return jnp.maximum(z, 0.0) + jnp.log1p(jnp.exp(-jnp.abs(z)))


def _sigmoid(z):
    return 0.5 * jnp.tanh(0.5 * z) + 0.5


def _rms(x, scale):
    return x * lax.rsqrt(jnp.mean(x * x, axis=-1, keepdims=True) + RMS_EPS) * scale


def _in_proj_kernel(x_ref, scale_ref, w_ref, wf_ref, proj_ref, f_ref, xn_ref):
    @pl.when(pl.program_id(1) == 0)
    def _():
        xn = _rms(x_ref[...], scale_ref[...]).astype(jnp.bfloat16)
        xn_ref[...] = xn
        f_ref[...] = jnp.dot(xn, wf_ref[...], preferred_element_type=jnp.float32)

    proj_ref[...] = jnp.dot(xn_ref[...], w_ref[...],
                            preferred_element_type=jnp.float32).astype(proj_ref.dtype)


def _in_proj(x2, scale, w, wf, tm):
    T = x2.shape[0]
    n_col = w.shape[1] // D_MODEL
    return pl.pallas_call(
        _in_proj_kernel,
        grid=(T // tm, n_col),
        in_specs=[
            pl.BlockSpec((tm, D_MODEL), lambda i, j: (i, 0)),
            pl.BlockSpec((1, D_MODEL), lambda i, j: (0, 0)),
            pl.BlockSpec((D_MODEL, D_MODEL), lambda i, j: (0, j)),
            pl.BlockSpec((D_MODEL, LANES), lambda i, j: (0, 0)),
        ],
        out_specs=[
            pl.BlockSpec((tm, D_MODEL), lambda i, j: (i, j)),
            pl.BlockSpec((tm, LANES), lambda i, j: (i, 0)),
        ],
        out_shape=[
            jax.ShapeDtypeStruct((T, w.shape[1]), jnp.bfloat16),
            jax.ShapeDtypeStruct((T, LANES), jnp.float32),
        ],
        scratch_shapes=[pltpu.VMEM((tm, D_MODEL), jnp.bfloat16)],
        compiler_params=_cparams(("parallel", "arbitrary")),
        name="in_proj",
    )(x2, scale, w, wf)


def _fgate_kernel(f_ref, fb_ref, c_ref, *, chunk):
    S = f_ref.shape[0]
    row = lax.broadcasted_iota(jnp.int32, (chunk, chunk), 0)
    col = lax.broadcasted_iota(jnp.int32, (chunk, chunk), 1)
    tri = (row >= col).astype(jnp.float32)
    carry = jnp.zeros((1, LANES), jnp.float32)
    for c in range(S // chunk):
        z = f_ref[pl.ds(c * chunk, chunk), :] + fb_ref[...]
        log_f = -_softplus(-z)
        cs = jnp.dot(tri, log_f, precision=lax.Precision.HIGHEST,
                     preferred_element_type=jnp.float32) + carry
        carry = cs[chunk - 1:chunk, :]
        c_ref[:, pl.ds(c * chunk, chunk)] = (cs * LOG2E).T[:N_HEADS, :]


def _fgate(f3, fb, chunk=256):
    B, S, _ = f3.shape
    return pl.pallas_call(
        functools.partial(_fgate_kernel, chunk=chunk),
        grid=(B,),
        in_specs=[
            pl.BlockSpec((None, S, LANES), lambda b: (b, 0, 0)),
            pl.BlockSpec((1, LANES), lambda b: (0, 0)),
        ],
        out_specs=pl.BlockSpec((None, N_HEADS, S), lambda b: (b, 0, 0)),
        out_shape=jax.ShapeDtypeStruct((B, N_HEADS, S), jnp.float32),
        compiler_params=_cparams(("parallel",)),
        name="fgate",
    )(f3, fb)


def _attn_kernel(q_ref, k_ref, v_ref, c_ref, o_ref, *, tq, tk):
    S = q_ref.shape[0]
    lane = lax.broadcasted_iota(jnp.int32, (1, LANES), 1)
    head0 = lane < HEAD_DIM
    row = lax.broadcasted_iota(jnp.int32, (HEADS_PER_SLAB * tq, tk), 0) % tq
    col = lax.broadcasted_iota(jnp.int32, (HEADS_PER_SLAB * tq, tk), 1)

    def chunk(ks, carry, q2, mask):
        m, l, acc = carry
        s = lax.dot_general(q2, k_ref[pl.ds(ks, tk), :], (((1,), (1,)), ((), ())),
                            preferred_element_type=jnp.float32)
        cb = c_ref[:, pl.ds(ks, tk)]
        s = jnp.concatenate([s[:tq] - cb[0:1], s[tq:] - cb[1:2]], axis=0)
        if mask is not None:
            s = jnp.where(mask, s, NEG_BIG)
        m_new = jnp.maximum(m, jnp.max(s, axis=-1, keepdims=True))
        alpha = jnp.exp2(m - m_new)
        p = jnp.exp2(s - m_new)
        l = alpha * l + jnp.sum(p, axis=-1, keepdims=True)
        pv = jnp.dot(p.astype(jnp.bfloat16), v_ref[pl.ds(ks, tk), :],
                     preferred_element_type=jnp.float32)
        return m_new, l, alpha * acc + pv

    for qi in range(S // tq):
        qs = qi * tq
        q = q_ref[pl.ds(qs, tq), :]
        zero = jnp.zeros_like(q)
        q2 = jnp.concatenate([jnp.where(head0, q, zero), jnp.where(head0, zero, q)], axis=0)
        carry = (jnp.full((HEADS_PER_SLAB * tq, 1), NEG_BIG, jnp.float32),
                 jnp.zeros((HEADS_PER_SLAB * tq, 1), jnp.float32),
                 jnp.zeros((HEADS_PER_SLAB * tq, LANES), jnp.float32))
        n_full = qs // tk
        n_chunks = -(-(qs + tq) // tk)
        for kc in range(n_chunks):
            mask = None if kc < n_full else (row + qs) >= (col + kc * tk)
            carry = chunk(kc * tk, carry, q2, mask)
        _, l, acc = carry
        out = acc / l
        o_ref[pl.ds(qs, tq), :] = jnp.where(head0, out[:tq], out[tq:]).astype(o_ref.dtype)


def _attention(proj3, c4, tq, tk):
    B, S, _ = proj3.shape
    slab = lambda c: pl.BlockSpec((None, S, LANES), lambda b, p: (b, 0, c * N_SLABS + p))
    return pl.pallas_call(
        functools.partial(_attn_kernel, tq=tq, tk=tk),
        grid=(B, N_SLABS),
        in_specs=[slab(COL_Q), slab(COL_K), slab(COL_V),
                  pl.BlockSpec((None, None, HEADS_PER_SLAB, S), lambda b, p: (b, p, 0, 0))],
        out_specs=pl.BlockSpec((None, S, LANES), lambda b, p: (b, 0, p)),
        out_shape=jax.ShapeDtypeStruct((B, S, D_MODEL), jnp.bfloat16),
        compiler_params=_cparams(("parallel", "parallel")),
        name="attn",
    )(proj3, proj3, proj3, c4)


def _rglru_kernel(x_ref, y_ref, cw_ref, cb_ref, wg_ref, ba_ref, bx_ref, lam_ref, o_ref,
                  xe_ref, a_ref, b_ref, h_ref, *, ts):
    HALO = 8
    s_idx = pl.program_id(1)

    @pl.when(s_idx == 0)
    def _():
        xe_ref[pl.ds(0, HALO), :] = jnp.zeros((HALO, D_MODEL), jnp.float32)
        h_ref[...] = jnp.zeros_like(h_ref)

    xe_ref[pl.ds(HALO, ts), :] = x_ref[...].astype(jnp.float32)
    conv = cb_ref[...] + cw_ref[CONV_W - 1:CONV_W, :] * xe_ref[pl.ds(HALO, ts), :]
    for j in range(CONV_W - 1):
        shift = CONV_W - 1 - j
        conv = conv + cw_ref[j:j + 1, :] * xe_ref[pl.ds(HALO - shift, ts), :]
    xe_ref[pl.ds(0, HALO), :] = xe_ref[pl.ds(ts, HALO), :]

    cb16 = conv.astype(jnp.bfloat16)
    sp = _softplus(-lam_ref[...])
    first = (lax.broadcasted_iota(jnp.int32, (ts, 1), 0) + s_idx * ts) == 0
    for g in range(N_LRU_GROUPS):
        cols = slice(g * LRU_GROUP, (g + 1) * LRU_GROUP)
        gates = jnp.dot(cb16[:, cols], wg_ref[g], preferred_element_type=jnp.float32)
        r = _sigmoid(gates[:, :LRU_GROUP] + ba_ref[:, cols])
        i = _sigmoid(gates[:, LRU_GROUP:] + bx_ref[:, cols])
        log_a = -LRU_C * r * sp[:, cols]
        a = jnp.exp(log_a)
        mult = jnp.where(first, 1.0, jnp.sqrt(1.0 - a * a))
        a_ref[:, cols] = a
        b_ref[:, cols] = mult * (i * conv[:, cols])

    SUB = 8
    rows = lax.broadcasted_iota(jnp.int32, (SUB, D_MODEL), 0)

    def slab(i, h_prev):
        start = pl.multiple_of(i * SUB, SUB)
        a = a_ref[pl.ds(start, SUB), :]
        b = b_ref[pl.ds(start, SUB), :]
        for d in (1, 2, 4):
            keep = rows >= d
            a_s = jnp.where(keep, pltpu.roll(a, d, 0), 1.0)
            b_s = jnp.where(keep, pltpu.roll(b, d, 0), 0.0)
            b = a * b_s + b
            a = a * a_s
        h = a * h_prev + b
        gate = jax.nn.gelu(y_ref[pl.ds(start, SUB), :].astype(jnp.float32))
        o_ref[pl.ds(start, SUB), :] = (h * gate).astype(o_ref.dtype)
        return jnp.broadcast_to(h[SUB - 1:SUB, :], (SUB, D_MODEL))

    h_ref[...] = lax.fori_loop(0, ts // SUB, slab, h_ref[...], unroll=4)


def _rglru(proj3, conv_w, conv_b, wg, ba, bx, lam, ts):
    B, S, _ = proj3.shape
    full = lambda shape: pl.BlockSpec(shape, lambda b, s: (0,) * len(shape))
    return pl.pallas_call(
        functools.partial(_rglru_kernel, ts=ts),
        grid=(B, S // ts),
        in_specs=[
            pl.BlockSpec((None, ts, D_MODEL), lambda b, s: (b, s, COL_XREC)),
            pl.BlockSpec((None, ts, D_MODEL), lambda b, s: (b, s, COL_YREC)),
            full((CONV_W, D_MODEL)),
            full((1, D_MODEL)),
            full((N_LRU_GROUPS, LRU_GROUP, 2 * LRU_GROUP)),
            full((1, D_MODEL)),
            full((1, D_MODEL)),
            full((1, D_MODEL)),
        ],
        out_specs=pl.BlockSpec((None, ts, D_MODEL), lambda b, s: (b, s, 0)),
        out_shape=jax.ShapeDtypeStruct((B, S, D_MODEL), jnp.bfloat16),
        scratch_shapes=[
            pltpu.VMEM((ts + 8, D_MODEL), jnp.float32),
            pltpu.VMEM((ts, D_MODEL), jnp.float32),
            pltpu.VMEM((ts, D_MODEL), jnp.float32),
            pltpu.VMEM((8, D_MODEL), jnp.float32),
        ],
        compiler_params=_cparams(("parallel", "arbitrary")),
        name="rglru",
    )(proj3, proj3, conv_w, conv_b, wg, ba, bx, lam)


def _merge_kernel(attn_ref, rec_ref, ga_ref, gr_ref, x_ref, wa_ref, wr_ref, wo_ref,
                  gba_ref, gbr_ref, n2_ref, rw_ref, rb_ref,
                  x1_ref, xn2_ref, idx_ref, gate_ref, rank_ref, cnt_ref, *, tm):
    pa = jnp.dot(attn_ref[...], wa_ref[...], preferred_element_type=jnp.float32)
    pr = jnp.dot(rec_ref[...], wr_ref[...], preferred_element_type=jnp.float32)
    g_a = _sigmoid(ga_ref[...].astype(jnp.float32) + gba_ref[...])
    g_r = _sigmoid(gr_ref[...].astype(jnp.float32) + gbr_ref[...])
    merged = (g_a * pa + g_r * pr).astype(jnp.bfloat16)
    x1 = x_ref[...] + jnp.dot(merged, wo_ref[...], preferred_element_type=jnp.float32)
    x1_ref[...] = x1
    xn2 = _rms(x1, n2_ref[...])
    xh = xn2.astype(jnp.bfloat16)
    xn2_ref[...] = xh

    xl = (xn2 - xh.astype(jnp.float32)).astype(jnp.bfloat16)
    part = jnp.dot(xh, rw_ref[...], preferred_element_type=jnp.float32)
    logits = (part[:, :LANES] + part[:, LANES:]
              + jnp.dot(xl, rw_ref[:, :LANES], preferred_element_type=jnp.float32)) + rb_ref[...]
    lane = lax.broadcasted_iota(jnp.int32, (tm, LANES), 1).astype(jnp.float32)
    idx_out = jnp.zeros((tm, LANES), jnp.float32)
    val_out = jnp.zeros((tm, LANES), jnp.float32)
    onehot = jnp.zeros((tm, LANES), jnp.float32)
    sel = []
    work = logits
    for j in range(TOP_K):
        m = jnp.max(work, axis=-1, keepdims=True)
        idx = jnp.min(jnp.where(work == m, lane, float(LANES)), axis=-1, keepdims=True)
        hit = lane == idx
        sel.append(hit)
        onehot = jnp.where(hit, 1.0, onehot)
        idx_out = jnp.where(lane == float(j), idx, idx_out)
        val_out = jnp.where(lane == float(j), m, val_out)
        work = jnp.where(hit, NEG_BIG, work)
    is_slot = lane < float(TOP_K)
    e = jnp.where(is_slot, jnp.exp(val_out - jnp.max(jnp.where(is_slot, val_out, NEG_BIG),
                                                      axis=-1, keepdims=True)), 0.0)
    gate_ref[...] = e / jnp.sum(e, axis=-1, keepdims=True)
    idx_ref[...] = idx_out.astype(jnp.int32)

    row = lax.broadcasted_iota(jnp.int32, (tm, tm), 0)
    col = lax.broadcasted_iota(jnp.int32, (tm, tm), 1)
    strict = (row > col).astype(jnp.bfloat16)
    before = jnp.dot(strict, onehot.astype(jnp.bfloat16), preferred_element_type=jnp.float32)
    rank_out = jnp.zeros((tm, LANES), jnp.float32)
    for j in range(TOP_K):
        rj = jnp.sum(jnp.where(sel[j], before, 0.0), axis=-1, keepdims=True)
        rank_out = jnp.where(lane == float(j), rj, rank_out)
    rank_ref[...] = rank_out.astype(jnp.int32)
    cnt_ref[...] = jnp.sum(onehot, axis=0, keepdims=True).astype(jnp.int32)


def _merge(attn2, rec2, proj2, x2, wa, wr, wo, gba, gbr, n2, rw, rb, tm):
    T = x2.shape[0]
    tok = lambda c: pl.BlockSpec((tm, D_MODEL), lambda i: (i, c))
    full = lambda shape: pl.BlockSpec(shape, lambda i: (0,) * len(shape))
    meta = pl.BlockSpec((tm, LANES), lambda i: (i, 0))
    return pl.pallas_call(
        functools.partial(_merge_kernel, tm=tm),
        grid=(T // tm,),
        in_specs=[tok(0), tok(0), tok(COL_GA), tok(COL_GR), tok(0),
                  full((D_MODEL, D_MODEL)), full((D_MODEL, D_MODEL)), full((D_MODEL, D_MODEL)),
                  full((1, D_MODEL)), full((1, D_MODEL)), full((1, D_MODEL)),
                  full((D_MODEL, 2 * LANES)), full((1, LANES))],
        out_specs=[tok(0), tok(0), meta, meta, meta,
                   pl.BlockSpec((None, 1, LANES), lambda i: (i, 0, 0))],
        out_shape=[
            jax.ShapeDtypeStruct((T, D_MODEL), jnp.float32),
            jax.ShapeDtypeStruct((T, D_MODEL), jnp.bfloat16),
            jax.ShapeDtypeStruct((T, LANES), jnp.int32),
            jax.ShapeDtypeStruct((T, LANES), jnp.float32),
            jax.ShapeDtypeStruct((T, LANES), jnp.int32),
            jax.ShapeDtypeStruct((T // tm, 1, LANES), jnp.int32),
        ],
        compiler_params=_cparams(("parallel",)),
        name="merge_router",
    )(attn2, rec2, proj2, proj2, x2, wa, wr, wo, gba, gbr, n2, rw, rb)


SEG = 16
SEG_PIECES = 6
FILL_GROUPS = 1 << (SEG_PIECES - 1)


def _segment_copies(src, src_off, dst, dst_off, n_seg, sem, wait):
    for k in range(SEG_PIECES):
        @pl.when(((n_seg >> k) & 1) == 1)
        def _(k=k):
            off = n_seg & ((1 << k) - 1)
            cp = pltpu.make_async_copy(src.at[pl.ds(src_off + off, 1 << k)],
                                       dst.at[pl.ds(dst_off + off, 1 << k)], sem)
            if wait:
                cp.wait()
            else:
                cp.start()


TABLE_LEAD = 2
TABLE_TAIL = 1


def _tile_copies(local_fn, hbm_fn, to_hbm, row, nseg_ref, lss_ref, gseg_ref, sem, wait):
    def per_expert(e, carry):
        k = row * N_EXPERTS + e
        local, hbm = (local_fn(), lss_ref[k]), (hbm_fn(), gseg_ref[k])
        (src, src_off), (dst, dst_off) = (local, hbm) if to_hbm else (hbm, local)
        _segment_copies(src, src_off, dst, dst_off, nseg_ref[k], sem, wait)
        return carry

    lax.fori_loop(0, N_EXPERTS, per_expert, 0)


def _local_positions(idx_ref, rank_ref, lss_ref):
    tm = idx_ref.shape[0]
    lane = lax.broadcasted_iota(jnp.int32, (tm, LANES), 1)
    idx = idx_ref[...]
    lss = lss_ref[...].astype(jnp.float32)
    pos = jnp.zeros((tm, LANES), jnp.float32)
    for j in range(TOP_K):
        start = jnp.sum(jnp.where(lane == idx[:, j:j + 1], lss, 0.0), axis=-1, keepdims=True)
        pos = jnp.where(lane == j, start, pos)
    return pos + rank_ref[...].astype(jnp.float32)


def _dispatch_kernel(nseg_ref, lss_s_ref, gseg_ref, fill_ref, xn_ref, idx_ref, rank_ref, lss_ref,
                     zero_ref, xs_ref, buf_ref, sem, *, rc):
    i = pl.program_id(0)
    tm = xn_ref.shape[0]
    r_loc = buf_ref.shape[1] * SEG

    @pl.when(i == 0)
    def _():
        fill_sem = sem.at[0]

        def fill_range(e, carry):
            lo = fill_ref[0, e]
            n = fill_ref[1, e] - lo
            n_big = n // FILL_GROUPS
            rem = n - n_big * FILL_GROUPS

            def big(b, wait):
                cp = pltpu.make_async_copy(
                    zero_ref, xs_ref.at[pl.ds(lo + b * FILL_GROUPS, FILL_GROUPS)], fill_sem)
                if wait:
                    cp.wait()
                else:
                    cp.start()

            lax.fori_loop(0, n_big, lambda b, c: (big(b, False), c)[1], 0)
            _segment_copies(zero_ref, 0, xs_ref, lo + n_big * FILL_GROUPS, rem, fill_sem, False)
            lax.fori_loop(0, n_big, lambda b, c: (big(b, True), c)[1], 0)
            _segment_copies(zero_ref, 0, xs_ref, lo + n_big * FILL_GROUPS, rem, fill_sem, True)
            return carry

        lax.fori_loop(0, N_EXPERTS + 1, fill_range, 0)

    slot = i % 2
    _tile_copies(lambda: buf_ref.at[slot], lambda: xs_ref, True, i, nseg_ref, lss_s_ref, gseg_ref,
                 sem.at[slot], wait=True)

    pos_t = _local_positions(idx_ref, rank_ref, lss_ref).T[:TOP_K, :]
    x = xn_ref[...]
    for c in range(r_loc // rc):
        rho = (lax.broadcasted_iota(jnp.int32, (rc, tm), 0) + c * rc).astype(jnp.float32)
        hit = rho == pos_t[0:1, :]
        for j in range(1, TOP_K):
            hit = hit | (rho == pos_t[j:j + 1, :])
        perm = jnp.where(hit, 1.0, 0.0).astype(jnp.bfloat16)
        buf_ref[slot, pl.ds(c * rc // SEG, rc // SEG)] = jnp.dot(
            perm, x, preferred_element_type=jnp.float32).astype(buf_ref.dtype).reshape(
                rc // SEG, SEG, D_MODEL)

    _tile_copies(lambda: buf_ref.at[1 - slot], lambda: xs_ref, True, i + 1, nseg_ref, lss_s_ref,
                 gseg_ref, sem.at[1 - slot], wait=False)

    @pl.when(i == pl.num_programs(0) - 1)
    def _():
        _tile_copies(lambda: buf_ref.at[1 - slot], lambda: xs_ref, True, i + 1, nseg_ref, lss_s_ref,
                     gseg_ref, sem.at[1 - slot], wait=True)


def _dispatch(nseg, lss_flat, gseg, fill, xn2, idx, rank, lss3, n_rows, tm, r_loc):
    n_tiles = xn2.shape[0] // tm
    zeros = jnp.zeros((FILL_GROUPS, SEG, D_MODEL), xn2.dtype)
    smem = pl.BlockSpec(memory_space=pltpu.SMEM)
    tile = lambda i: jnp.minimum(i, n_tiles - 1)
    meta = pl.BlockSpec((tm, LANES), lambda i: (tile(i), 0))
    return pl.pallas_call(
        functools.partial(_dispatch_kernel, rc=256),
        grid=(n_tiles + 1,),
        in_specs=[smem, smem, smem, smem,
                  pl.BlockSpec((tm, D_MODEL), lambda i: (tile(i), 0)), meta, meta,
                  pl.BlockSpec((None, 1, LANES), lambda i: (tile(i), 0, 0)),
                  pl.BlockSpec((FILL_GROUPS, SEG, D_MODEL), lambda i: (0, 0, 0))],
        out_specs=pl.BlockSpec(memory_space=pl.ANY),
        out_shape=jax.ShapeDtypeStruct((n_rows // SEG, SEG, D_MODEL), xn2.dtype),
        scratch_shapes=[pltpu.VMEM((2, r_loc // SEG, SEG, D_MODEL), xn2.dtype),
                        pltpu.SemaphoreType.DMA((2,))],
        compiler_params=_cparams(("arbitrary",)),
        name="dispatch",
    )(nseg, lss_flat, gseg, fill, xn2, idx, rank, lss3, zeros)


def _moe_kernel(be_ref, nu_ref, xs_ref, w1_ref, b1_ref, w2_ref, b2_ref, ys_ref, w1b_ref, w2b_ref):
    i = pl.program_id(0)
    prev = be_ref[jnp.maximum(i - 1, 0)]

    @pl.when((i == 0) | (be_ref[i] != prev))
    def _():
        w1b_ref[...] = w1_ref[...].astype(jnp.bfloat16)
        w2b_ref[...] = w2_ref[...].astype(jnp.bfloat16)

    @pl.when(i < nu_ref[0])
    def _():
        h = jnp.dot(xs_ref[...], w1b_ref[...],
                    preferred_element_type=jnp.float32) + b1_ref[...]
        d = h.shape[1] // 2
        glu = jnp.minimum(h[:, :d], SWIGLU_LIMIT)
        lin = jnp.clip(h[:, d:], -SWIGLU_LIMIT, SWIGLU_LIMIT)
        act = glu * _sigmoid(SWIGLU_ALPHA * glu) * (lin + 1.0)
        y = jnp.dot(act.astype(jnp.bfloat16), w2b_ref[...],
                    preferred_element_type=jnp.float32) + b2_ref[...]
        ys_ref[...] = y.astype(ys_ref.dtype)

    @pl.when(i >= nu_ref[0])
    def _():
        ys_ref[...] = jnp.zeros_like(ys_ref)


def _moe(block_expert, n_used, xs, w1, b1, w2, b2, te):
    n_blocks = xs.shape[0] // te
    d_ff2 = w1.shape[2]
    grid_spec = pltpu.PrefetchScalarGridSpec(
        num_scalar_prefetch=2,
        grid=(n_blocks,),
        in_specs=[
            pl.BlockSpec((te, D_MODEL), lambda i, be, nu: (jnp.clip(i, 0, jnp.maximum(nu[0] - 1, 0)), 0)),
            pl.BlockSpec((None, D_MODEL, d_ff2), lambda i, be, nu: (be[i], 0, 0)),
            pl.BlockSpec((None, 1, d_ff2), lambda i, be, nu: (be[i], 0, 0)),
            pl.BlockSpec((None, d_ff2 // 2, D_MODEL), lambda i, be, nu: (be[i], 0, 0)),
            pl.BlockSpec((None, 1, D_MODEL), lambda i, be, nu: (be[i], 0, 0)),
        ],
        out_specs=pl.BlockSpec((te, D_MODEL), lambda i, be, nu: (i, 0)),
        scratch_shapes=[pltpu.VMEM((D_MODEL, d_ff2), jnp.bfloat16),
                        pltpu.VMEM((d_ff2 // 2, D_MODEL), jnp.bfloat16)],
    )
    return pl.pallas_call(
        _moe_kernel,
        grid_spec=grid_spec,
        out_shape=jax.ShapeDtypeStruct(xs.shape, xs.dtype),
        compiler_params=_cparams(("arbitrary",)),
        name="moe_experts",
    )(block_expert, n_used, xs, w1, b1, w2, b2)


def _combine_kernel(nseg_ref, lss_s_ref, gseg_ref, idx_ref, rank_ref, lss_ref, gate_ref, x1_ref,
                    fs_ref, ys_ref, o_ref, buf_ref, sem, *, rc):
    i = pl.program_id(0)
    tm = x1_ref.shape[0]
    r_loc = buf_ref.shape[1] * SEG
    slot = i % 2
    row = i + TABLE_LEAD

    @pl.when(i == 0)
    def _():
        buf_ref[...] = jnp.zeros_like(buf_ref)
        _tile_copies(lambda: buf_ref.at[0], lambda: ys_ref, False, TABLE_LEAD, nseg_ref, lss_s_ref,
                     gseg_ref, sem.at[0], wait=False)

    _tile_copies(lambda: buf_ref.at[1 - slot], lambda: ys_ref, False, row + 1, nseg_ref, lss_s_ref,
                 gseg_ref, sem.at[1 - slot], wait=False)
    _tile_copies(lambda: buf_ref.at[slot], lambda: ys_ref, False, row, nseg_ref, lss_s_ref,
                 gseg_ref, sem.at[slot], wait=True)

    pos = _local_positions(idx_ref, rank_ref, lss_ref)
    gates = gate_ref[...]
    acc = x1_ref[...]
    for c in range(r_loc // rc):
        rho = (lax.broadcasted_iota(jnp.int32, (tm, rc), 1) + c * rc).astype(jnp.float32)
        g = jnp.zeros((tm, rc), jnp.float32)
        for j in range(TOP_K):
            g = g + jnp.where(rho == pos[:, j:j + 1], gates[:, j:j + 1], 0.0)
        rows = buf_ref[slot, pl.ds(c * rc // SEG, rc // SEG)].reshape(rc, D_MODEL)
        acc = acc + jnp.dot(g.astype(jnp.bfloat16), rows, preferred_element_type=jnp.float32)
    o_ref[...] = _rms(acc, fs_ref[...])


def _combine(nseg, lss_flat, gseg, idx, rank, lss3, gates, x1, fscale, ys, tm, r_loc):
    T = x1.shape[0]
    smem = pl.BlockSpec(memory_space=pltpu.SMEM)
    meta = pl.BlockSpec((tm, LANES), lambda i: (i, 0))
    return pl.pallas_call(
        functools.partial(_combine_kernel, rc=256),
        grid=(T // tm,),
        in_specs=[smem, smem, smem, meta, meta,
                  pl.BlockSpec((None, 1, LANES), lambda i: (i, 0, 0)), meta,
                  pl.BlockSpec((tm, D_MODEL), lambda i: (i, 0)),
                  pl.BlockSpec((1, D_MODEL), lambda i: (0, 0)),
                  pl.BlockSpec(memory_space=pl.ANY)],
        out_specs=pl.BlockSpec((tm, D_MODEL), lambda i: (i, 0)),
        out_shape=jax.ShapeDtypeStruct((T, D_MODEL), jnp.float32),
        scratch_shapes=[pltpu.VMEM((2, r_loc // SEG, SEG, D_MODEL), ys.dtype),
                        pltpu.SemaphoreType.DMA((2,))],
        compiler_params=_cparams(("arbitrary",)),
        name="combine",
    )(nseg, lss_flat, gseg, idx, rank, lss3, gates, x1, fscale, ys)


def _block_diag_groups(wa, wx):
    per = LRU_GROUP // LRU_BLOCK

    def bd(w):
        w = w.reshape(N_LRU_GROUPS, per, LRU_BLOCK, LRU_BLOCK)
        eye = jnp.eye(per, dtype=w.dtype)
        return jnp.einsum('gnij,nm->gnimj', w, eye).reshape(N_LRU_GROUPS, LRU_GROUP, LRU_GROUP)

    return jnp.concatenate([bd(wa), bd(wx)], axis=-1).astype(jnp.bfloat16)


def _pick_tile(n, pref):
    t = min(pref, n)
    while n % t:
        t //= 2
    return t


def _layer(x, norm1_scale, w_in, f_bias, gate_bias, conv_w, conv_b, rg_wa, rg_ba, rg_wx, rg_bx,
           rg_lambda, w_attn_proj, w_rec_proj, w_out, norm2_scale, router_w, router_b,
           moe_w1, moe_b1, moe_w2, moe_b2, final_scale):
    B, S, D = x.shape
    T = B * S
    A = T * TOP_K
    x2 = x.reshape(T, D)
    row = lambda v: v.reshape(1, -1).astype(jnp.float32)

    aw = N_HEADS * HEAD_DIM
    o_f = 3 * aw
    o_x = o_f + N_HEADS
    w_main = jnp.concatenate([w_in[:, :aw] * (HEAD_DIM ** -0.5 * LOG2E), w_in[:, aw:o_f], w_in[:, o_x:]],
                             axis=1).astype(jnp.bfloat16)
    w_f = jnp.pad(w_in[:, o_f:o_x], ((0, 0), (0, LANES - N_HEADS))).astype(jnp.bfloat16)
    f_b = jnp.pad(row(f_bias), ((0, 0), (0, LANES - N_HEADS)))

    proj, f_logit = _in_proj(x2, row(norm1_scale), w_main, w_f, _pick_tile(T, 2048))
    proj3 = proj.reshape(B, S, PROJ_COLS)

    c = _fgate(f_logit.reshape(B, S, LANES), f_b, chunk=_pick_tile(S, 256))
    attn = _attention(proj3, c.reshape(B, N_SLABS, HEADS_PER_SLAB, S), _pick_tile(S, 256),
                      _pick_tile(S, 256))

    rec = _rglru(proj3, conv_w.astype(jnp.float32), row(conv_b), _block_diag_groups(rg_wa, rg_wx),
                 row(rg_ba), row(rg_bx), row(rg_lambda), _pick_tile(S, 512))

    rw32 = jnp.pad(router_w.astype(jnp.float32), ((0, 0), (0, LANES - N_EXPERTS)))
    rw_hi = rw32.astype(jnp.bfloat16)
    rw = jnp.concatenate([rw_hi, (rw32 - rw_hi.astype(jnp.float32)).astype(jnp.bfloat16)], axis=1)
    rb = jnp.pad(row(router_b), ((0, 0), (0, LANES - N_EXPERTS)), constant_values=NEG_BIG)
    gb = row(gate_bias)
    tm = _pick_tile(T, 512)
    x1, xn2, top_idx, gates, rank, tile_cnt = _merge(
        attn.reshape(T, D), rec.reshape(T, D), proj, x2,
        w_attn_proj.astype(jnp.bfloat16), w_rec_proj.astype(jnp.bfloat16), w_out.astype(jnp.bfloat16),
        gb[:, :D], gb[:, D:], row(norm2_scale), rw, rb, tm)

    n_tiles = T // tm
    te = 512
    n_blocks = -(-(A + (SEG - 1) * N_EXPERTS * n_tiles) // te) + N_EXPERTS
    n_rows = n_blocks * te
    r_loc = -(-(TOP_K * tm + (SEG - 1) * N_EXPERTS) // 256) * 256
    cnt = tile_cnt[:, 0, :N_EXPERTS]
    seg = (cnt + SEG - 1) // SEG * SEG
    lss = jnp.cumsum(seg, axis=1) - seg
    total = jnp.sum(seg, axis=0)
    padded = (total + te - 1) // te * te
    pad_end = jnp.cumsum(padded)
    pad_start = pad_end - padded
    gseg = pad_start[None, :] + jnp.cumsum(seg, axis=0) - seg
    n_used = (pad_end[-1] // te).astype(jnp.int32)
    blk = jnp.arange(n_blocks, dtype=jnp.int32)
    be = jnp.sum(pad_end[None, :] <= (blk * te)[:, None], axis=-1)
    block_expert = jnp.minimum(jnp.minimum(be, be[jnp.maximum(n_used - 1, 0)]),
                               N_EXPERTS - 1).astype(jnp.int32)
    fill = jnp.stack([jnp.concatenate([pad_start + total, pad_end[-1:]]),
                      jnp.concatenate([pad_end, jnp.full((1,), n_rows, pad_end.dtype)])]).astype(jnp.int32)
    flat = lambda a: jnp.pad(a, ((TABLE_LEAD, TABLE_TAIL), (0, 0))).reshape(-1).astype(jnp.int32)
    lss3 = jnp.pad(lss, ((0, 0), (0, LANES - N_EXPERTS))).reshape(n_tiles, 1, LANES).astype(jnp.int32)

    xs = _dispatch(flat(seg // SEG), flat(lss // SEG), flat(gseg // SEG), fill // SEG, xn2, top_idx,
                   rank, lss3, n_rows, tm, r_loc)
    ys = _moe(block_expert, n_used.reshape(1), xs.reshape(n_rows, D), moe_w1, moe_b1.reshape(N_EXPERTS, 1, -1),
              moe_w2, moe_b2.reshape(N_EXPERTS, 1, -1), te)
    out = _combine(flat(seg // SEG), flat(lss // SEG), flat(gseg // SEG), top_idx, rank, lss3, gates, x1,
                   row(final_scale), ys.reshape(n_rows // SEG, SEG, D), tm, r_loc)
    return out.reshape(B, S, D)


def kernel(x, norm1_scale, w_in, f_bias, gate_bias, conv_w, conv_b, rg_wa, rg_ba, rg_wx, rg_bx, rg_lambda, w_attn_proj, w_rec_proj, w_out, norm2_scale, router_w, router_b, moe_w1, moe_b1, moe_w2, moe_b2, final_norm_scale):
    depth = norm1_scale.shape[0]
    assert depth == 1, "the final rmsnorm is fused into the single layer's combine step"
    l = 0
    return _layer(x, norm1_scale[l], w_in[l], f_bias[l], gate_bias[l], conv_w[l], conv_b[l],
                  rg_wa[l], rg_ba[l], rg_wx[l], rg_bx[l], rg_lambda[l], w_attn_proj[l],
                  w_rec_proj[l], w_out[l], norm2_scale[l], router_w[l], router_b[l],
                  moe_w1[l], moe_b1[l], moe_w2[l], moe_b2[l], final_norm_scale)
```

```python
import functools

import jax
import jax.numpy as jnp
from jax import lax
from jax.experimental import pallas as pl
from jax.experimental.pallas import tpu as pltpu

D_MODEL = 1024
N_HEADS = 16
HEAD_DIM = 64
LANES = 128
HEADS_PER_SLAB = LANES // HEAD_DIM
N_SLABS = D_MODEL // LANES
N_LRU_BLOCKS = 16
LRU_BLOCK = 64
LRU_GROUP = 256
N_LRU_GROUPS = D_MODEL // LRU_GROUP
CONV_W = 4
LRU_C = 8.0
N_EXPERTS = 32
TOP_K = 4
SWIGLU_ALPHA = 1.702
SWIGLU_LIMIT = 7.0
RMS_EPS = 1e-6
NEG_BIG = -1e30
LOG2E = 1.4426950408889634

PROJ_COLS = 7 * D_MODEL
COL_Q, COL_K, COL_V, COL_XREC, COL_YREC, COL_GA, COL_GR = range(7)

VMEM_LIMIT = 56 * 1024 * 1024
MXU_DIM = 256

TILE_IN_PROJ = 2048
TILE_ATTN_Q = 256
TILE_ATTN_K = MXU_DIM
TILE_SCAN = 512
TILE_TOKENS = 512
TILE_EXPERT = 512


def _cparams(sem):
    return pltpu.CompilerParams(dimension_semantics=sem, vmem_limit_bytes=VMEM_LIMIT)


def _softplus(z):
    return jnp.maximum(z, 0.0) + jnp.log1p(jnp.exp(-jnp.abs(z)))


def _sigmoid(z):
    return 0.5 * jnp.tanh(0.5 * z) + 0.5


def _rms(x, scale):
    return x * lax.rsqrt(jnp.mean(x * x, axis=-1, keepdims=True) + RMS_EPS) * scale


def _in_proj_kernel(x_ref, scale_ref, w_ref, wf_ref, proj_ref, f_ref, xn_ref):
    @pl.when(pl.program_id(1) == 0)
    def _():
        xn = _rms(x_ref[...], scale_ref[...]).astype(jnp.bfloat16)
        xn_ref[...] = xn
        f_ref[...] = jnp.dot(xn, wf_ref[...], preferred_element_type=jnp.float32)

    proj_ref[...] = jnp.dot(xn_ref[...], w_ref[...],
                            preferred_element_type=jnp.float32).astype(proj_ref.dtype)


def _in_proj(x2, scale, w, wf, tm):
    T = x2.shape[0]
    n_col = w.shape[1] // D_MODEL
    return pl.pallas_call(
        _in_proj_kernel,
        grid=(T // tm, n_col),
        in_specs=[
            pl.BlockSpec((tm, D_MODEL), lambda i, j: (i, 0)),
            pl.BlockSpec((1, D_MODEL), lambda i, j: (0, 0)),
            pl.BlockSpec((D_MODEL, D_MODEL), lambda i, j: (0, j)),
            pl.BlockSpec((D_MODEL, LANES), lambda i, j: (0, 0)),
        ],
        out_specs=[
            pl.BlockSpec((tm, D_MODEL), lambda i, j: (i, j)),
            pl.BlockSpec((tm, LANES), lambda i, j: (i, 0)),
        ],
        out_shape=[
            jax.ShapeDtypeStruct((T, w.shape[1]), jnp.bfloat16),
            jax.ShapeDtypeStruct((T, LANES), jnp.float32),
        ],
        scratch_shapes=[pltpu.VMEM((tm, D_MODEL), jnp.bfloat16)],
        compiler_params=_cparams(("parallel", "arbitrary")),
        name="in_proj",
    )(x2, scale, w, wf)


def _fgate_kernel(f_ref, fb_ref, c_ref, *, chunk):
    S = f_ref.shape[0]
    row = lax.broadcasted_iota(jnp.int32, (chunk, chunk), 0)
    col = lax.broadcasted_iota(jnp.int32, (chunk, chunk), 1)
    tri = (row >= col).astype(jnp.float32)
    carry = jnp.zeros((1, LANES), jnp.float32)
    for c in range(S // chunk):
        z = f_ref[pl.ds(c * chunk, chunk), :] + fb_ref[...]
        log_f = -_softplus(-z)
        cs = jnp.dot(tri, log_f, precision=lax.Precision.HIGHEST,
                     preferred_element_type=jnp.float32) + carry
        carry = cs[chunk - 1:chunk, :]
        c_ref[:, pl.ds(c * chunk, chunk)] = (cs * LOG2E).T[:N_HEADS, :]


def _fgate(f3, fb, chunk):
    B, S, _ = f3.shape
    return pl.pallas_call(
        functools.partial(_fgate_kernel, chunk=chunk),
        grid=(B,),
        in_specs=[
            pl.BlockSpec((None, S, LANES), lambda b: (b, 0, 0)),
            pl.BlockSpec((1, LANES), lambda b: (0, 0)),
        ],
        out_specs=pl.BlockSpec((None, N_HEADS, S), lambda b: (b, 0, 0)),
        out_shape=jax.ShapeDtypeStruct((B, N_HEADS, S), jnp.float32),
        compiler_params=_cparams(("parallel",)),
        name="fgate",
    )(f3, fb)


def _attn_kernel(q_ref, k_ref, v_ref, c_ref, o_ref, *, tq, tk):
    S = q_ref.shape[0]
    lane = lax.broadcasted_iota(jnp.int32, (1, LANES), 1)
    head0 = lane < HEAD_DIM
    row = lax.broadcasted_iota(jnp.int32, (HEADS_PER_SLAB * tq, tk), 0) % tq
    col = lax.broadcasted_iota(jnp.int32, (HEADS_PER_SLAB * tq, tk), 1)

    def chunk(ks, carry, q2, mask):
        m, l, acc = carry
        s = lax.dot_general(q2, k_ref[pl.ds(ks, tk), :], (((1,), (1,)), ((), ())),
                            preferred_element_type=jnp.float32)
        cb = c_ref[:, pl.ds(ks, tk)]
        s = jnp.concatenate([s[:tq] - cb[0:1], s[tq:] - cb[1:2]], axis=0)
        if mask is not None:
            s = jnp.where(mask, s, NEG_BIG)
        m_new = jnp.maximum(m, jnp.max(s, axis=-1, keepdims=True))
        alpha = jnp.exp2(m - m_new)
        p = jnp.exp2(s - m_new)
        l = alpha * l + jnp.sum(p, axis=-1, keepdims=True)
        pv = jnp.dot(p.astype(jnp.bfloat16), v_ref[pl.ds(ks, tk), :],
                     preferred_element_type=jnp.float32)
        return m_new, l, alpha * acc + pv

    for qi in range(S // tq):
        qs = qi * tq
        q = q_ref[pl.ds(qs, tq), :]
        zero = jnp.zeros_like(q)
        q2 = jnp.concatenate([jnp.where(head0, q, zero), jnp.where(head0, zero, q)], axis=0)
        carry = (jnp.full((HEADS_PER_SLAB * tq, 1), NEG_BIG, jnp.float32),
                 jnp.zeros((HEADS_PER_SLAB * tq, 1), jnp.float32),
                 jnp.zeros((HEADS_PER_SLAB * tq, LANES), jnp.float32))
        n_full = qs // tk
        n_chunks = -(-(qs + tq) // tk)
        for kc in range(n_chunks):
            mask = None if kc < n_full else (row + qs) >= (col + kc * tk)
            carry = chunk(kc * tk, carry, q2, mask)
        _, l, acc = carry
        out = acc / l
        o_ref[pl.ds(qs, tq), :] = jnp.where(head0, out[:tq], out[tq:]).astype(o_ref.dtype)


def _attention(proj3, c4, tq, tk):
    B, S, _ = proj3.shape
    slab = lambda c: pl.BlockSpec((None, S, LANES), lambda b, p: (b, 0, c * N_SLABS + p))
    return pl.pallas_call(
        functools.partial(_attn_kernel, tq=tq, tk=tk),
        grid=(B, N_SLABS),
        in_specs=[slab(COL_Q), slab(COL_K), slab(COL_V),
                  pl.BlockSpec((None, None, HEADS_PER_SLAB, S), lambda b, p: (b, p, 0, 0))],
        out_specs=pl.BlockSpec((None, S, LANES), lambda b, p: (b, 0, p)),
        out_shape=jax.ShapeDtypeStruct((B, S, D_MODEL), jnp.bfloat16),
        compiler_params=_cparams(("parallel", "parallel")),
        name="attn",
    )(proj3, proj3, proj3, c4)


def _rglru_kernel(x_ref, y_ref, cw_ref, cb_ref, wg_ref, ba_ref, bx_ref, lam_ref, o_ref,
                  xe_ref, a_ref, b_ref, h_ref, *, ts):
    HALO = 8
    s_idx = pl.program_id(1)

    @pl.when(s_idx == 0)
    def _():
        xe_ref[pl.ds(0, HALO), :] = jnp.zeros((HALO, D_MODEL), jnp.float32)
        h_ref[...] = jnp.zeros_like(h_ref)

    xe_ref[pl.ds(HALO, ts), :] = x_ref[...].astype(jnp.float32)
    conv = cb_ref[...] + cw_ref[CONV_W - 1:CONV_W, :] * xe_ref[pl.ds(HALO, ts), :]
    for j in range(CONV_W - 1):
        shift = CONV_W - 1 - j
        conv = conv + cw_ref[j:j + 1, :] * xe_ref[pl.ds(HALO - shift, ts), :]
    xe_ref[pl.ds(0, HALO), :] = xe_ref[pl.ds(ts, HALO), :]

    cb16 = conv.astype(jnp.bfloat16)
    sp = _softplus(-lam_ref[...])
    first = (lax.broadcasted_iota(jnp.int32, (ts, 1), 0) + s_idx * ts) == 0
    for g in range(N_LRU_GROUPS):
        cols = slice(g * LRU_GROUP, (g + 1) * LRU_GROUP)
        gates = jnp.dot(cb16[:, cols], wg_ref[g], preferred_element_type=jnp.float32)
        r = _sigmoid(gates[:, :LRU_GROUP] + ba_ref[:, cols])
        i = _sigmoid(gates[:, LRU_GROUP:] + bx_ref[:, cols])
        log_a = -LRU_C * r * sp[:, cols]
        a = jnp.exp(log_a)
        mult = jnp.where(first, 1.0, jnp.sqrt(1.0 - a * a))
        a_ref[:, cols] = a
        b_ref[:, cols] = mult * (i * conv[:, cols])

    SUB = 8
    rows = lax.broadcasted_iota(jnp.int32, (SUB, D_MODEL), 0)

    def slab(i, h_prev):
        start = pl.multiple_of(i * SUB, SUB)
        a = a_ref[pl.ds(start, SUB), :]
        b = b_ref[pl.ds(start, SUB), :]
        for d in (1, 2, 4):
            keep = rows >= d
            a_s = jnp.where(keep, pltpu.roll(a, d, 0), 1.0)
            b_s = jnp.where(keep, pltpu.roll(b, d, 0), 0.0)
            b = a * b_s + b
            a = a * a_s
        h = a * h_prev + b
        gate = jax.nn.gelu(y_ref[pl.ds(start, SUB), :].astype(jnp.float32))
        o_ref[pl.ds(start, SUB), :] = (h * gate).astype(o_ref.dtype)
        return jnp.broadcast_to(h[SUB - 1:SUB, :], (SUB, D_MODEL))

    h_ref[...] = lax.fori_loop(0, ts // SUB, slab, h_ref[...], unroll=4)


def _rglru(proj3, conv_w, conv_b, wg, ba, bx, lam, ts):
    B, S, _ = proj3.shape
    full = lambda shape: pl.BlockSpec(shape, lambda b, s: (0,) * len(shape))
    return pl.pallas_call(
        functools.partial(_rglru_kernel, ts=ts),
        grid=(B, S // ts),
        in_specs=[
            pl.BlockSpec((None, ts, D_MODEL), lambda b, s: (b, s, COL_XREC)),
            pl.BlockSpec((None, ts, D_MODEL), lambda b, s: (b, s, COL_YREC)),
            full((CONV_W, D_MODEL)),
            full((1, D_MODEL)),
            full((N_LRU_GROUPS, LRU_GROUP, 2 * LRU_GROUP)),
            full((1, D_MODEL)),
            full((1, D_MODEL)),
            full((1, D_MODEL)),
        ],
        out_specs=pl.BlockSpec((None, ts, D_MODEL), lambda b, s: (b, s, 0)),
        out_shape=jax.ShapeDtypeStruct((B, S, D_MODEL), jnp.bfloat16),
        scratch_shapes=[
            pltpu.VMEM((ts + 8, D_MODEL), jnp.float32),
            pltpu.VMEM((ts, D_MODEL), jnp.float32),
            pltpu.VMEM((ts, D_MODEL), jnp.float32),
            pltpu.VMEM((8, D_MODEL), jnp.float32),
        ],
        compiler_params=_cparams(("parallel", "arbitrary")),
        name="rglru",
    )(proj3, proj3, conv_w, conv_b, wg, ba, bx, lam)


def _merge_kernel(attn_ref, rec_ref, ga_ref, gr_ref, x_ref, wa_ref, wr_ref, wo_ref,
                  gba_ref, gbr_ref, n2_ref, rw_ref, rb_ref,
                  x1_ref, xn2_ref, idx_ref, gate_ref, rank_ref, cnt_ref, *, tm):
    pa = jnp.dot(attn_ref[...], wa_ref[...], preferred_element_type=jnp.float32)
    pr = jnp.dot(rec_ref[...], wr_ref[...], preferred_element_type=jnp.float32)
    g_a = _sigmoid(ga_ref[...].astype(jnp.float32) + gba_ref[...])
    g_r = _sigmoid(gr_ref[...].astype(jnp.float32) + gbr_ref[...])
    merged = (g_a * pa + g_r * pr).astype(jnp.bfloat16)
    x1 = x_ref[...] + jnp.dot(merged, wo_ref[...], preferred_element_type=jnp.float32)
    x1_ref[...] = x1
    xn2 = _rms(x1, n2_ref[...])
    xh = xn2.astype(jnp.bfloat16)
    xn2_ref[...] = xh

    xl = (xn2 - xh.astype(jnp.float32)).astype(jnp.bfloat16)
    part = jnp.dot(xh, rw_ref[...], preferred_element_type=jnp.float32)
    logits = (part[:, :LANES] + part[:, LANES:]
              + jnp.dot(xl, rw_ref[:, :LANES], preferred_element_type=jnp.float32)) + rb_ref[...]
    lane = lax.broadcasted_iota(jnp.int32, (tm, LANES), 1).astype(jnp.float32)
    idx_out = jnp.zeros((tm, LANES), jnp.float32)
    val_out = jnp.zeros((tm, LANES), jnp.float32)
    onehot = jnp.zeros((tm, LANES), jnp.float32)
    sel = []
    work = logits
    for j in range(TOP_K):
        m = jnp.max(work, axis=-1, keepdims=True)
        idx = jnp.min(jnp.where(work == m, lane, float(LANES)), axis=-1, keepdims=True)
        hit = lane == idx
        sel.append(hit)
        onehot = jnp.where(hit, 1.0, onehot)
        idx_out = jnp.where(lane == float(j), idx, idx_out)
        val_out = jnp.where(lane == float(j), m, val_out)
        work = jnp.where(hit, NEG_BIG, work)
    is_slot = lane < float(TOP_K)
    e = jnp.where(is_slot, jnp.exp(val_out - jnp.max(jnp.where(is_slot, val_out, NEG_BIG),
                                                      axis=-1, keepdims=True)), 0.0)
    gate_ref[...] = e / jnp.sum(e, axis=-1, keepdims=True)
    idx_ref[...] = idx_out.astype(jnp.int32)

    row = lax.broadcasted_iota(jnp.int32, (tm, tm), 0)
    col = lax.broadcasted_iota(jnp.int32, (tm, tm), 1)
    strict = (row > col).astype(jnp.bfloat16)
    before = jnp.dot(strict, onehot.astype(jnp.bfloat16), preferred_element_type=jnp.float32)
    rank_out = jnp.zeros((tm, LANES), jnp.float32)
    for j in range(TOP_K):
        rj = jnp.sum(jnp.where(sel[j], before, 0.0), axis=-1, keepdims=True)
        rank_out = jnp.where(lane == float(j), rj, rank_out)
    rank_ref[...] = rank_out.astype(jnp.int32)
    cnt_ref[...] = jnp.sum(onehot, axis=0, keepdims=True).astype(jnp.int32)


def _merge(attn2, rec2, proj2, x2, wa, wr, wo, gba, gbr, n2, rw, rb, tm):
    T = x2.shape[0]
    tok = lambda c: pl.BlockSpec((tm, D_MODEL), lambda i: (i, c))
    full = lambda shape: pl.BlockSpec(shape, lambda i: (0,) * len(shape))
    meta = pl.BlockSpec((tm, LANES), lambda i: (i, 0))
    return pl.pallas_call(
        functools.partial(_merge_kernel, tm=tm),
        grid=(T // tm,),
        in_specs=[tok(0), tok(0), tok(COL_GA), tok(COL_GR), tok(0),
                  full((D_MODEL, D_MODEL)), full((D_MODEL, D_MODEL)), full((D_MODEL, D_MODEL)),
                  full((1, D_MODEL)), full((1, D_MODEL)), full((1, D_MODEL)),
                  full((D_MODEL, 2 * LANES)), full((1, LANES))],
        out_specs=[tok(0), tok(0), meta, meta, meta,
                   pl.BlockSpec((None, 1, LANES), lambda i: (i, 0, 0))],
        out_shape=[
            jax.ShapeDtypeStruct((T, D_MODEL), jnp.float32),
            jax.ShapeDtypeStruct((T, D_MODEL), jnp.bfloat16),
            jax.ShapeDtypeStruct((T, LANES), jnp.int32),
            jax.ShapeDtypeStruct((T, LANES), jnp.float32),
            jax.ShapeDtypeStruct((T, LANES), jnp.int32),
            jax.ShapeDtypeStruct((T // tm, 1, LANES), jnp.int32),
        ],
        compiler_params=_cparams(("parallel",)),
        name="merge_router",
    )(attn2, rec2, proj2, proj2, x2, wa, wr, wo, gba, gbr, n2, rw, rb)


SEG = 16
TABLE_LEAD = 2
TABLE_TAIL = 1


def _group_copy(local, u, hbm, g, to_hbm, sem):
    src, dst = (local.at[pl.ds(u, 1)], hbm.at[pl.ds(g, 1)])
    return pltpu.make_async_copy(src, dst, sem) if to_hbm else pltpu.make_async_copy(dst, src, sem)


def _tile_copies(local, hbm, to_hbm, row, ngrp_ref, dst_ref, sem, wait):
    groups = local.shape[0]

    def body(u, carry):
        if wait:
            _group_copy(local, 0, hbm, 0, to_hbm, sem).wait()
        else:
            _group_copy(local, u, hbm, dst_ref[row * groups + u], to_hbm, sem).start()
        return carry

    lax.fori_loop(0, ngrp_ref[row], body, 0)


def _local_positions(idx_ref, rank_ref, lss_ref):
    tm = idx_ref.shape[0]
    lane = lax.broadcasted_iota(jnp.int32, (tm, LANES), 1)
    idx = idx_ref[...]
    lss = lss_ref[...].astype(jnp.float32)
    pos = jnp.zeros((tm, LANES), jnp.float32)
    for j in range(TOP_K):
        start = jnp.sum(jnp.where(lane == idx[:, j:j + 1], lss, 0.0), axis=-1, keepdims=True)
        pos = jnp.where(lane == j, start, pos)
    return pos + rank_ref[...].astype(jnp.float32)


def _dispatch_kernel(ngrp_ref, dst_ref, fill_ref, xn_ref, idx_ref, rank_ref, lss_ref,
                     zero_ref, xs_ref, buf_ref, sem, *, rc):
    i = pl.program_id(0)
    tm = xn_ref.shape[0]
    r_loc = buf_ref.shape[1] * SEG

    @pl.when(i == 0)
    def _():
        def fill_range(e, carry):
            def fill(g, c, wait):
                cp = _group_copy(zero_ref, 0, xs_ref, 0 if wait else g, True, sem.at[0])
                if wait:
                    cp.wait()
                else:
                    cp.start()
                return c

            lax.fori_loop(fill_ref[0, e], fill_ref[1, e], functools.partial(fill, wait=False), 0)
            lax.fori_loop(fill_ref[0, e], fill_ref[1, e], functools.partial(fill, wait=True), 0)
            return carry

        lax.fori_loop(0, N_EXPERTS + 1, fill_range, 0)

    slot = i % 2
    _tile_copies(buf_ref.at[slot], xs_ref, True, i, ngrp_ref, dst_ref, sem.at[slot], wait=True)

    pos_t = _local_positions(idx_ref, rank_ref, lss_ref).T[:TOP_K, :]
    x = xn_ref[...]
    for c in range(r_loc // rc):
        rho = (lax.broadcasted_iota(jnp.int32, (rc, tm), 0) + c * rc).astype(jnp.float32)
        hit = rho == pos_t[0:1, :]
        for j in range(1, TOP_K):
            hit = hit | (rho == pos_t[j:j + 1, :])
        perm = jnp.where(hit, 1.0, 0.0).astype(jnp.bfloat16)
        buf_ref[slot, pl.ds(c * rc // SEG, rc // SEG)] = jnp.dot(
            perm, x, preferred_element_type=jnp.float32).astype(buf_ref.dtype).reshape(
                rc // SEG, SEG, D_MODEL)

    _tile_copies(buf_ref.at[1 - slot], xs_ref, True, i + 1, ngrp_ref, dst_ref, sem.at[1 - slot],
                 wait=False)

    @pl.when(i == pl.num_programs(0) - 1)
    def _():
        _tile_copies(buf_ref.at[1 - slot], xs_ref, True, i + 1, ngrp_ref, dst_ref, sem.at[1 - slot],
                     wait=True)


def _dispatch(ngrp, dst, fill, xn2, idx, rank, lss3, n_rows, tm, r_loc):
    n_tiles = xn2.shape[0] // tm
    zeros = jnp.zeros((1, SEG, D_MODEL), xn2.dtype)
    smem = pl.BlockSpec(memory_space=pltpu.SMEM)
    tile = lambda i: jnp.minimum(i, n_tiles - 1)
    meta = pl.BlockSpec((tm, LANES), lambda i: (tile(i), 0))
    return pl.pallas_call(
        functools.partial(_dispatch_kernel, rc=MXU_DIM),
        grid=(n_tiles + 1,),
        in_specs=[smem, smem, smem,
                  pl.BlockSpec((tm, D_MODEL), lambda i: (tile(i), 0)), meta, meta,
                  pl.BlockSpec((None, 1, LANES), lambda i: (tile(i), 0, 0)),
                  pl.BlockSpec((1, SEG, D_MODEL), lambda i: (0, 0, 0))],
        out_specs=pl.BlockSpec(memory_space=pl.ANY),
        out_shape=jax.ShapeDtypeStruct((n_rows // SEG, SEG, D_MODEL), xn2.dtype),
        scratch_shapes=[pltpu.VMEM((2, r_loc // SEG, SEG, D_MODEL), xn2.dtype),
                        pltpu.SemaphoreType.DMA((2,))],
        compiler_params=_cparams(("arbitrary",)),
        name="dispatch",
    )(ngrp, dst, fill, xn2, idx, rank, lss3, zeros)


def _moe_kernel(be_ref, nu_ref, xs_ref, w1_ref, b1_ref, w2_ref, b2_ref, ys_ref, w1b_ref, w2b_ref):
    i = pl.program_id(0)
    prev = be_ref[jnp.maximum(i - 1, 0)]

    @pl.when((i == 0) | (be_ref[i] != prev))
    def _():
        w1b_ref[...] = w1_ref[...].astype(jnp.bfloat16)
        w2b_ref[...] = w2_ref[...].astype(jnp.bfloat16)

    @pl.when(i < nu_ref[0])
    def _():
        h = jnp.dot(xs_ref[...], w1b_ref[...],
                    preferred_element_type=jnp.float32) + b1_ref[...]
        d = h.shape[1] // 2
        glu = jnp.minimum(h[:, :d], SWIGLU_LIMIT)
        lin = jnp.clip(h[:, d:], -SWIGLU_LIMIT, SWIGLU_LIMIT)
        act = glu * _sigmoid(SWIGLU_ALPHA * glu) * (lin + 1.0)
        y = jnp.dot(act.astype(jnp.bfloat16), w2b_ref[...],
                    preferred_element_type=jnp.float32) + b2_ref[...]
        ys_ref[...] = y.astype(ys_ref.dtype)

    @pl.when(i >= nu_ref[0])
    def _():
        ys_ref[...] = jnp.zeros_like(ys_ref)


def _moe(block_expert, n_used, xs, w1, b1, w2, b2, te):
    n_blocks = xs.shape[0] // te
    d_ff2 = w1.shape[2]
    grid_spec = pltpu.PrefetchScalarGridSpec(
        num_scalar_prefetch=2,
        grid=(n_blocks,),
        in_specs=[
            pl.BlockSpec((te, D_MODEL), lambda i, be, nu: (jnp.clip(i, 0, jnp.maximum(nu[0] - 1, 0)), 0)),
            pl.BlockSpec((None, D_MODEL, d_ff2), lambda i, be, nu: (be[i], 0, 0)),
            pl.BlockSpec((None, 1, d_ff2), lambda i, be, nu: (be[i], 0, 0)),
            pl.BlockSpec((None, d_ff2 // 2, D_MODEL), lambda i, be, nu: (be[i], 0, 0)),
            pl.BlockSpec((None, 1, D_MODEL), lambda i, be, nu: (be[i], 0, 0)),
        ],
        out_specs=pl.BlockSpec((te, D_MODEL), lambda i, be, nu: (i, 0)),
        scratch_shapes=[pltpu.VMEM((D_MODEL, d_ff2), jnp.bfloat16),
                        pltpu.VMEM((d_ff2 // 2, D_MODEL), jnp.bfloat16)],
    )
    return pl.pallas_call(
        _moe_kernel,
        grid_spec=grid_spec,
        out_shape=jax.ShapeDtypeStruct(xs.shape, xs.dtype),
        compiler_params=_cparams(("arbitrary",)),
        name="moe_experts",
    )(block_expert, n_used, xs, w1, b1, w2, b2)


def _combine_kernel(ngrp_ref, dst_ref, idx_ref, rank_ref, lss_ref, gate_ref, x1_ref,
                    fs_ref, ys_ref, o_ref, buf_ref, sem, *, rc):
    i = pl.program_id(0)
    tm = x1_ref.shape[0]
    r_loc = buf_ref.shape[1] * SEG
    slot = i % 2
    row = i + TABLE_LEAD

    @pl.when(i == 0)
    def _():
        buf_ref[...] = jnp.zeros_like(buf_ref)
        _tile_copies(buf_ref.at[0], ys_ref, False, TABLE_LEAD, ngrp_ref, dst_ref, sem.at[0], wait=False)

    _tile_copies(buf_ref.at[1 - slot], ys_ref, False, row + 1, ngrp_ref, dst_ref, sem.at[1 - slot],
                 wait=False)
    _tile_copies(buf_ref.at[slot], ys_ref, False, row, ngrp_ref, dst_ref, sem.at[slot], wait=True)

    pos = _local_positions(idx_ref, rank_ref, lss_ref)
    gates = gate_ref[...]
    acc = x1_ref[...]
    for c in range(r_loc // rc):
        rho = (lax.broadcasted_iota(jnp.int32, (tm, rc), 1) + c * rc).astype(jnp.float32)
        g = jnp.zeros((tm, rc), jnp.float32)
        for j in range(TOP_K):
            g = g + jnp.where(rho == pos[:, j:j + 1], gates[:, j:j + 1], 0.0)
        rows = buf_ref[slot, pl.ds(c * rc // SEG, rc // SEG)].reshape(rc, D_MODEL)
        acc = acc + jnp.dot(g.astype(jnp.bfloat16), rows, preferred_element_type=jnp.float32)
    o_ref[...] = _rms(acc, fs_ref[...])


def _combine(ngrp, dst, idx, rank, lss3, gates, x1, fscale, ys, tm, r_loc):
    T = x1.shape[0]
    smem = pl.BlockSpec(memory_space=pltpu.SMEM)
    meta = pl.BlockSpec((tm, LANES), lambda i: (i, 0))
    return pl.pallas_call(
        functools.partial(_combine_kernel, rc=MXU_DIM),
        grid=(T // tm,),
        in_specs=[smem, smem, meta, meta,
                  pl.BlockSpec((None, 1, LANES), lambda i: (i, 0, 0)), meta,
                  pl.BlockSpec((tm, D_MODEL), lambda i: (i, 0)),
                  pl.BlockSpec((1, D_MODEL), lambda i: (0, 0)),
                  pl.BlockSpec(memory_space=pl.ANY)],
        out_specs=pl.BlockSpec((tm, D_MODEL), lambda i: (i, 0)),
        out_shape=jax.ShapeDtypeStruct((T, D_MODEL), jnp.float32),
        scratch_shapes=[pltpu.VMEM((2, r_loc // SEG, SEG, D_MODEL), ys.dtype),
                        pltpu.SemaphoreType.DMA((2,))],
        compiler_params=_cparams(("arbitrary",)),
        name="combine",
    )(ngrp, dst, idx, rank, lss3, gates, x1, fscale, ys)


def _block_diag_groups(wa, wx):
    per = LRU_GROUP // LRU_BLOCK

    def bd(w):
        w = w.reshape(N_LRU_GROUPS, per, LRU_BLOCK, LRU_BLOCK)
        eye = jnp.eye(per, dtype=w.dtype)
        return jnp.einsum('gnij,nm->gnimj', w, eye).reshape(N_LRU_GROUPS, LRU_GROUP, LRU_GROUP)

    return jnp.concatenate([bd(wa), bd(wx)], axis=-1).astype(jnp.bfloat16)


def _pick_tile(n, pref):
    t = min(pref, n)
    while n % t:
        t //= 2
    return t


def _layer(x, norm1_scale, w_in, f_bias, gate_bias, conv_w, conv_b, rg_wa, rg_ba, rg_wx, rg_bx,
           rg_lambda, w_attn_proj, w_rec_proj, w_out, norm2_scale, router_w, router_b,
           moe_w1, moe_b1, moe_w2, moe_b2, final_scale):
    B, S, D = x.shape
    T = B * S
    A = T * TOP_K
    x2 = x.reshape(T, D)
    row = lambda v: v.reshape(1, -1).astype(jnp.float32)

    aw = N_HEADS * HEAD_DIM
    o_f = 3 * aw
    o_x = o_f + N_HEADS
    w_main = jnp.concatenate([w_in[:, :aw] * (HEAD_DIM ** -0.5 * LOG2E), w_in[:, aw:o_f], w_in[:, o_x:]],
                             axis=1).astype(jnp.bfloat16)
    w_f = jnp.pad(w_in[:, o_f:o_x], ((0, 0), (0, LANES - N_HEADS))).astype(jnp.bfloat16)
    f_b = jnp.pad(row(f_bias), ((0, 0), (0, LANES - N_HEADS)))

    proj, f_logit = _in_proj(x2, row(norm1_scale), w_main, w_f, _pick_tile(T, TILE_IN_PROJ))
    proj3 = proj.reshape(B, S, PROJ_COLS)

    c = _fgate(f_logit.reshape(B, S, LANES), f_b, chunk=_pick_tile(S, MXU_DIM))
    attn = _attention(proj3, c.reshape(B, N_SLABS, HEADS_PER_SLAB, S), _pick_tile(S, TILE_ATTN_Q),
                      _pick_tile(S, TILE_ATTN_K))

    rec = _rglru(proj3, conv_w.astype(jnp.float32), row(conv_b), _block_diag_groups(rg_wa, rg_wx),
                 row(rg_ba), row(rg_bx), row(rg_lambda), _pick_tile(S, TILE_SCAN))

    rw32 = jnp.pad(router_w.astype(jnp.float32), ((0, 0), (0, LANES - N_EXPERTS)))
    rw_hi = rw32.astype(jnp.bfloat16)
    rw = jnp.concatenate([rw_hi, (rw32 - rw_hi.astype(jnp.float32)).astype(jnp.bfloat16)], axis=1)
    rb = jnp.pad(row(router_b), ((0, 0), (0, LANES - N_EXPERTS)), constant_values=NEG_BIG)
    gb = row(gate_bias)
    tm = _pick_tile(T, TILE_TOKENS)
    x1, xn2, top_idx, gates, rank, tile_cnt = _merge(
        attn.reshape(T, D), rec.reshape(T, D), proj, x2,
        w_attn_proj.astype(jnp.bfloat16), w_rec_proj.astype(jnp.bfloat16), w_out.astype(jnp.bfloat16),
        gb[:, :D], gb[:, D:], row(norm2_scale), rw, rb, tm)

    n_tiles = T // tm
    te = TILE_EXPERT
    n_blocks = -(-(A + (SEG - 1) * N_EXPERTS * n_tiles) // te) + N_EXPERTS
    n_rows = n_blocks * te
    r_loc = -(-(TOP_K * tm + (SEG - 1) * N_EXPERTS) // MXU_DIM) * MXU_DIM
    cnt = tile_cnt[:, 0, :N_EXPERTS]
    seg = (cnt + SEG - 1) // SEG * SEG
    lss = jnp.cumsum(seg, axis=1) - seg
    total = jnp.sum(seg, axis=0)
    padded = (total + te - 1) // te * te
    pad_end = jnp.cumsum(padded)
    pad_start = pad_end - padded
    gseg = pad_start[None, :] + jnp.cumsum(seg, axis=0) - seg
    n_used = (pad_end[-1] // te).astype(jnp.int32)
    blk = jnp.arange(n_blocks, dtype=jnp.int32)
    be = jnp.sum(pad_end[None, :] <= (blk * te)[:, None], axis=-1)
    block_expert = jnp.minimum(jnp.minimum(be, be[jnp.maximum(n_used - 1, 0)]),
                               N_EXPERTS - 1).astype(jnp.int32)
    fill = jnp.stack([jnp.concatenate([pad_start + total, pad_end[-1:]]),
                      jnp.concatenate([pad_end, jnp.full((1,), n_rows, pad_end.dtype)])])
    lss3 = jnp.pad(lss, ((0, 0), (0, LANES - N_EXPERTS))).reshape(n_tiles, 1, LANES).astype(jnp.int32)

    u_row = jnp.arange(r_loc // SEG, dtype=cnt.dtype) * SEG
    owner = jnp.sum((lss + seg)[:, None, :] <= u_row[None, :, None], axis=-1)
    shift = jnp.sum(jnp.where(owner[:, :, None] == jnp.arange(N_EXPERTS), (gseg - lss)[:, None, :], 0),
                    axis=-1)
    dst = jnp.pad((shift + u_row) // SEG, ((TABLE_LEAD, TABLE_TAIL), (0, 0))).reshape(-1).astype(jnp.int32)
    ngrp = jnp.pad(jnp.sum(seg, axis=1) // SEG, (TABLE_LEAD, TABLE_TAIL)).astype(jnp.int32)

    xs = _dispatch(ngrp, dst, (fill // SEG).astype(jnp.int32), xn2, top_idx, rank, lss3, n_rows, tm, r_loc)
    ys = _moe(block_expert, n_used.reshape(1), xs.reshape(n_rows, D), moe_w1, moe_b1.reshape(N_EXPERTS, 1, -1),
              moe_w2, moe_b2.reshape(N_EXPERTS, 1, -1), te)
    out = _combine(ngrp, dst, top_idx, rank, lss3, gates, x1, row(final_scale),
                   ys.reshape(n_rows // SEG, SEG, D), tm, r_loc)
    return out.reshape(B, S, D)


def kernel(x, norm1_scale, w_in, f_bias, gate_bias, conv_w, conv_b, rg_wa, rg_ba, rg_wx, rg_bx, rg_lambda, w_attn_proj, w_rec_proj, w_out, norm2_scale, router_w, router_b, moe_w1, moe_b1, moe_w2, moe_b2, final_norm_scale):
    depth = norm1_scale.shape[0]
    assert depth == 1, "the final rmsnorm is fused into the single layer's combine step"
    l = 0
    return _layer(x, norm1_scale[l], w_in[l], f_bias[l], gate_bias[l], conv_w[l], conv_b[l],
                  rg_wa[l], rg_ba[l], rg_wx[l], rg_bx[l], rg_lambda[l], w_attn_proj[l],
                  w_rec_proj[l], w_out[l], norm2_scale[l], router_w[l], router_b[l],
                  moe_w1[l], moe_b1[l], moe_w2[l], moe_b2[l], final_norm_scale)
```

```python
import functools

import jax
import jax.numpy as jnp
from jax import lax
from jax.experimental import pallas as pl
from jax.experimental.pallas import tpu as pltpu

D_MODEL = 1024
N_HEADS = 16
HEAD_DIM = 64
LANES = 128
MXU_DIM = 256
HEADS_PER_SLAB = LANES // HEAD_DIM
N_SLABS = D_MODEL // LANES
N_LRU_BLOCKS = 16
LRU_BLOCK = 64
LRU_GROUP = MXU_DIM
N_LRU_GROUPS = D_MODEL // LRU_GROUP
CONV_W = 4
LRU_C = 8.0
N_EXPERTS = 32
TOP_K = 4
SWIGLU_ALPHA = 1.702
SWIGLU_LIMIT = 7.0
RMS_EPS = 1e-6
NEG_BIG = -1e30
LOG2E = 1.4426950408889634

PROJ_COLS = 7 * D_MODEL
COL_Q, COL_K, COL_V, COL_XREC, COL_YREC, COL_GA, COL_GR = range(7)

VMEM_LIMIT = 56 * 1024 * 1024

TILE_IN_PROJ = 2048
TILE_ATTN_Q = 256
TILE_ATTN_K = MXU_DIM
TILE_SCAN = 512
TILE_TOKENS = 512
TILE_EXPERT = 512


def _cparams(sem):
    return pltpu.CompilerParams(dimension_semantics=sem, vmem_limit_bytes=VMEM_LIMIT)


def _softplus(z):
    return jnp.maximum(z, 0.0) + jnp.log1p(jnp.exp(-jnp.abs(z)))


def _sigmoid(z):
    return 0.5 * jnp.tanh(0.5 * z) + 0.5


def _rms(x, scale):
    return x * lax.rsqrt(jnp.mean(x * x, axis=-1, keepdims=True) + RMS_EPS) * scale


def _in_proj_kernel(x_ref, scale_ref, w_ref, wf_ref, proj_ref, f_ref, xn_ref):
    @pl.when(pl.program_id(1) == 0)
    def _():
        xn = _rms(x_ref[...], scale_ref[...]).astype(jnp.bfloat16)
        xn_ref[...] = xn
        f_ref[...] = jnp.dot(xn, wf_ref[...], preferred_element_type=jnp.float32)

    proj_ref[...] = jnp.dot(xn_ref[...], w_ref[...],
                            preferred_element_type=jnp.float32).astype(proj_ref.dtype)


def _in_proj(x2, scale, w, wf, tm):
    T = x2.shape[0]
    n_col = w.shape[1] // D_MODEL
    return pl.pallas_call(
        _in_proj_kernel,
        grid=(T // tm, n_col),
        in_specs=[
            pl.BlockSpec((tm, D_MODEL), lambda i, j: (i, 0)),
            pl.BlockSpec((1, D_MODEL), lambda i, j: (0, 0)),
            pl.BlockSpec((D_MODEL, D_MODEL), lambda i, j: (0, j)),
            pl.BlockSpec((D_MODEL, LANES), lambda i, j: (0, 0)),
        ],
        out_specs=[
            pl.BlockSpec((tm, D_MODEL), lambda i, j: (i, j)),
            pl.BlockSpec((tm, LANES), lambda i, j: (i, 0)),
        ],
        out_shape=[
            jax.ShapeDtypeStruct((T, w.shape[1]), jnp.bfloat16),
            jax.ShapeDtypeStruct((T, LANES), jnp.float32),
        ],
        scratch_shapes=[pltpu.VMEM((tm, D_MODEL), jnp.bfloat16)],
        compiler_params=_cparams(("parallel", "arbitrary")),
        name="in_proj",
    )(x2, scale, w, wf)


def _fgate_kernel(f_ref, fb_ref, c_ref, *, chunk):
    S = f_ref.shape[0]
    row = lax.broadcasted_iota(jnp.int32, (chunk, chunk), 0)
    col = lax.broadcasted_iota(jnp.int32, (chunk, chunk), 1)
    tri = (row >= col).astype(jnp.float32)
    carry = jnp.zeros((1, LANES), jnp.float32)
    for c in range(S // chunk):
        z = f_ref[pl.ds(c * chunk, chunk), :] + fb_ref[...]
        log_f = -_softplus(-z)
        cs = jnp.dot(tri, log_f, precision=lax.Precision.HIGHEST,
                     preferred_element_type=jnp.float32) + carry
        carry = cs[chunk - 1:chunk, :]
        c_ref[:, pl.ds(c * chunk, chunk)] = (cs * LOG2E).T[:N_HEADS, :]


def _fgate(f3, fb, chunk):
    B, S, _ = f3.shape
    return pl.pallas_call(
        functools.partial(_fgate_kernel, chunk=chunk),
        grid=(B,),
        in_specs=[
            pl.BlockSpec((None, S, LANES), lambda b: (b, 0, 0)),
            pl.BlockSpec((1, LANES), lambda b: (0, 0)),
        ],
        out_specs=pl.BlockSpec((None, N_HEADS, S), lambda b: (b, 0, 0)),
        out_shape=jax.ShapeDtypeStruct((B, N_HEADS, S), jnp.float32),
        compiler_params=_cparams(("parallel",)),
        name="fgate",
    )(f3, fb)


def _attn_kernel(q_ref, k_ref, v_ref, c_ref, o_ref, *, tq, tk):
    S = q_ref.shape[0]
    lane = lax.broadcasted_iota(jnp.int32, (1, LANES), 1)
    head0 = lane < HEAD_DIM
    row = lax.broadcasted_iota(jnp.int32, (HEADS_PER_SLAB * tq, tk), 0) % tq
    col = lax.broadcasted_iota(jnp.int32, (HEADS_PER_SLAB * tq, tk), 1)

    def chunk(ks, carry, q2, mask):
        m, l, acc = carry
        s = lax.dot_general(q2, k_ref[pl.ds(ks, tk), :], (((1,), (1,)), ((), ())),
                            preferred_element_type=jnp.float32)
        cb = c_ref[:, pl.ds(ks, tk)]
        s = jnp.concatenate([s[:tq] - cb[0:1], s[tq:] - cb[1:2]], axis=0)
        if mask is not None:
            s = jnp.where(mask, s, NEG_BIG)
        m_new = jnp.maximum(m, jnp.max(s, axis=-1, keepdims=True))
        alpha = jnp.exp2(m - m_new)
        p = jnp.exp2(s - m_new)
        l = alpha * l + sum(p[:, j * LANES:(j + 1) * LANES] for j in range(tk // LANES))
        pv = jnp.dot(p.astype(jnp.bfloat16), v_ref[pl.ds(ks, tk), :],
                     preferred_element_type=jnp.float32)
        return m_new, l, alpha * acc + pv

    for qi in range(S // tq):
        qs = qi * tq
        q = q_ref[pl.ds(qs, tq), :]
        zero = jnp.zeros_like(q)
        q2 = jnp.concatenate([jnp.where(head0, q, zero), jnp.where(head0, zero, q)], axis=0)
        carry = (jnp.full((HEADS_PER_SLAB * tq, 1), NEG_BIG, jnp.float32),
                 jnp.zeros((HEADS_PER_SLAB * tq, LANES), jnp.float32),
                 jnp.zeros((HEADS_PER_SLAB * tq, LANES), jnp.float32))
        n_full = qs // tk
        n_chunks = -(-(qs + tq) // tk)
        for kc in range(n_chunks):
            mask = None if kc < n_full else (row + qs) >= (col + kc * tk)
            carry = chunk(kc * tk, carry, q2, mask)
        _, l, acc = carry
        out = acc / jnp.sum(l, axis=-1, keepdims=True)
        o_ref[pl.ds(qs, tq), :] = jnp.where(head0, out[:tq], out[tq:]).astype(o_ref.dtype)


def _attention(proj3, c4, tq, tk):
    B, S, _ = proj3.shape
    slab = lambda c: pl.BlockSpec((None, S, LANES), lambda b, p: (b, 0, c * N_SLABS + p))
    return pl.pallas_call(
        functools.partial(_attn_kernel, tq=tq, tk=tk),
        grid=(B, N_SLABS),
        in_specs=[slab(COL_Q), slab(COL_K), slab(COL_V),
                  pl.BlockSpec((None, None, HEADS_PER_SLAB, S), lambda b, p: (b, p, 0, 0))],
        out_specs=pl.BlockSpec((None, S, LANES), lambda b, p: (b, 0, p)),
        out_shape=jax.ShapeDtypeStruct((B, S, D_MODEL), jnp.bfloat16),
        compiler_params=_cparams(("parallel", "parallel")),
        name="attn",
    )(proj3, proj3, proj3, c4)


def _rglru_kernel(x_ref, y_ref, cw_ref, cb_ref, wg_ref, ba_ref, bx_ref, lam_ref, o_ref,
                  xe_ref, a_ref, b_ref, h_ref, *, ts):
    HALO = 8
    s_idx = pl.program_id(1)

    @pl.when(s_idx == 0)
    def _():
        xe_ref[pl.ds(0, HALO), :] = jnp.zeros((HALO, D_MODEL), jnp.float32)
        h_ref[...] = jnp.zeros_like(h_ref)

    xe_ref[pl.ds(HALO, ts), :] = x_ref[...].astype(jnp.float32)
    conv = cb_ref[...] + cw_ref[CONV_W - 1:CONV_W, :] * xe_ref[pl.ds(HALO, ts), :]
    for j in range(CONV_W - 1):
        shift = CONV_W - 1 - j
        conv = conv + cw_ref[j:j + 1, :] * xe_ref[pl.ds(HALO - shift, ts), :]
    xe_ref[pl.ds(0, HALO), :] = xe_ref[pl.ds(ts, HALO), :]

    cb16 = conv.astype(jnp.bfloat16)
    sp = _softplus(-lam_ref[...])
    first = (lax.broadcasted_iota(jnp.int32, (ts, 1), 0) + s_idx * ts) == 0
    for g in range(N_LRU_GROUPS):
        cols = slice(g * LRU_GROUP, (g + 1) * LRU_GROUP)
        gates = jnp.dot(cb16[:, cols], wg_ref[g], preferred_element_type=jnp.float32)
        r = _sigmoid(gates[:, :LRU_GROUP] + ba_ref[:, cols])
        i = _sigmoid(gates[:, LRU_GROUP:] + bx_ref[:, cols])
        log_a = -LRU_C * r * sp[:, cols]
        a = jnp.exp(log_a)
        mult = jnp.where(first, 1.0, jnp.sqrt(1.0 - a * a))
        a_ref[:, cols] = a
        b_ref[:, cols] = mult * (i * conv[:, cols])

    SUB = 8
    rows = lax.broadcasted_iota(jnp.int32, (SUB, D_MODEL), 0)

    def slab(i, h_prev):
        start = pl.multiple_of(i * SUB, SUB)
        a = a_ref[pl.ds(start, SUB), :]
        b = b_ref[pl.ds(start, SUB), :]
        for d in (1, 2, 4):
            keep = rows >= d
            a_s = jnp.where(keep, pltpu.roll(a, d, 0), 1.0)
            b_s = jnp.where(keep, pltpu.roll(b, d, 0), 0.0)
            b = a * b_s + b
            a = a * a_s
        h = a * h_prev + b
        gate = jax.nn.gelu(y_ref[pl.ds(start, SUB), :].astype(jnp.float32))
        o_ref[pl.ds(start, SUB), :] = (h * gate).astype(o_ref.dtype)
        return jnp.broadcast_to(h[SUB - 1:SUB, :], (SUB, D_MODEL))

    h_ref[...] = lax.fori_loop(0, ts // SUB, slab, h_ref[...], unroll=4)


def _rglru(proj3, conv_w, conv_b, wg, ba, bx, lam, ts):
    B, S, _ = proj3.shape
    full = lambda shape: pl.BlockSpec(shape, lambda b, s: (0,) * len(shape))
    return pl.pallas_call(
        functools.partial(_rglru_kernel, ts=ts),
        grid=(B, S // ts),
        in_specs=[
            pl.BlockSpec((None, ts, D_MODEL), lambda b, s: (b, s, COL_XREC)),
            pl.BlockSpec((None, ts, D_MODEL), lambda b, s: (b, s, COL_YREC)),
            full((CONV_W, D_MODEL)),
            full((1, D_MODEL)),
            full((N_LRU_GROUPS, LRU_GROUP, 2 * LRU_GROUP)),
            full((1, D_MODEL)),
            full((1, D_MODEL)),
            full((1, D_MODEL)),
        ],
        out_specs=pl.BlockSpec((None, ts, D_MODEL), lambda b, s: (b, s, 0)),
        out_shape=jax.ShapeDtypeStruct((B, S, D_MODEL), jnp.bfloat16),
        scratch_shapes=[
            pltpu.VMEM((ts + 8, D_MODEL), jnp.float32),
            pltpu.VMEM((ts, D_MODEL), jnp.float32),
            pltpu.VMEM((ts, D_MODEL), jnp.float32),
            pltpu.VMEM((8, D_MODEL), jnp.float32),
        ],
        compiler_params=_cparams(("parallel", "arbitrary")),
        name="rglru",
    )(proj3, proj3, conv_w, conv_b, wg, ba, bx, lam)


def _merge_kernel(attn_ref, rec_ref, ga_ref, gr_ref, x_ref, wa_ref, wr_ref, wo_ref,
                  gba_ref, gbr_ref, n2_ref, rw_ref, rb_ref,
                  x1_ref, xn2_ref, idx_ref, gate_ref, rank_ref, cnt_ref, *, tm):
    pa = jnp.dot(attn_ref[...], wa_ref[...], preferred_element_type=jnp.float32)
    pr = jnp.dot(rec_ref[...], wr_ref[...], preferred_element_type=jnp.float32)
    g_a = _sigmoid(ga_ref[...].astype(jnp.float32) + gba_ref[...])
    g_r = _sigmoid(gr_ref[...].astype(jnp.float32) + gbr_ref[...])
    merged = (g_a * pa + g_r * pr).astype(jnp.bfloat16)
    x1 = x_ref[...] + jnp.dot(merged, wo_ref[...], preferred_element_type=jnp.float32)
    x1_ref[...] = x1
    xn2 = _rms(x1, n2_ref[...])
    xh = xn2.astype(jnp.bfloat16)
    xn2_ref[...] = xh

    xl = (xn2 - xh.astype(jnp.float32)).astype(jnp.bfloat16)
    part = jnp.dot(xh, rw_ref[...], preferred_element_type=jnp.float32)
    logits = (part[:, :LANES] + part[:, LANES:]
              + jnp.dot(xl, rw_ref[:, :LANES], preferred_element_type=jnp.float32)) + rb_ref[...]
    lane = lax.broadcasted_iota(jnp.int32, (tm, LANES), 1).astype(jnp.float32)
    idx_out = jnp.zeros((tm, LANES), jnp.float32)
    val_out = jnp.zeros((tm, LANES), jnp.float32)
    onehot = jnp.zeros((tm, LANES), jnp.float32)
    sel = []
    work = logits
    for j in range(TOP_K):
        m = jnp.max(work, axis=-1, keepdims=True)
        idx = jnp.min(jnp.where(work == m, lane, float(LANES)), axis=-1, keepdims=True)
        hit = lane == idx
        sel.append(hit)
        onehot = jnp.where(hit, 1.0, onehot)
        idx_out = jnp.where(lane == float(j), idx, idx_out)
        val_out = jnp.where(lane == float(j), m, val_out)
        work = jnp.where(hit, NEG_BIG, work)
    is_slot = lane < float(TOP_K)
    e = jnp.where(is_slot, jnp.exp(val_out - jnp.max(jnp.where(is_slot, val_out, NEG_BIG),
                                                      axis=-1, keepdims=True)), 0.0)
    gate_ref[...] = e / jnp.sum(e, axis=-1, keepdims=True)
    idx_ref[...] = idx_out.astype(jnp.int32)

    row = lax.broadcasted_iota(jnp.int32, (tm, tm), 0)
    col = lax.broadcasted_iota(jnp.int32, (tm, tm), 1)
    strict = (row > col).astype(jnp.bfloat16)
    before = jnp.dot(strict, onehot.astype(jnp.bfloat16), preferred_element_type=jnp.float32)
    rank_out = jnp.zeros((tm, LANES), jnp.float32)
    for j in range(TOP_K):
        rj = jnp.sum(jnp.where(sel[j], before, 0.0), axis=-1, keepdims=True)
        rank_out = jnp.where(lane == float(j), rj, rank_out)
    rank_ref[...] = rank_out.astype(jnp.int32)
    cnt_ref[...] = jnp.sum(onehot, axis=0, keepdims=True).astype(jnp.int32)


def _merge(attn2, rec2, proj2, x2, wa, wr, wo, gba, gbr, n2, rw, rb, tm):
    T = x2.shape[0]
    tok = lambda c: pl.BlockSpec((tm, D_MODEL), lambda i: (i, c))
    full = lambda shape: pl.BlockSpec(shape, lambda i: (0,) * len(shape))
    meta = pl.BlockSpec((tm, LANES), lambda i: (i, 0))
    return pl.pallas_call(
        functools.partial(_merge_kernel, tm=tm),
        grid=(T // tm,),
        in_specs=[tok(0), tok(0), tok(COL_GA), tok(COL_GR), tok(0),
                  full((D_MODEL, D_MODEL)), full((D_MODEL, D_MODEL)), full((D_MODEL, D_MODEL)),
                  full((1, D_MODEL)), full((1, D_MODEL)), full((1, D_MODEL)),
                  full((D_MODEL, 2 * LANES)), full((1, LANES))],
        out_specs=[tok(0), tok(0), meta, meta, meta,
                   pl.BlockSpec((None, 1, LANES), lambda i: (i, 0, 0))],
        out_shape=[
            jax.ShapeDtypeStruct((T, D_MODEL), jnp.float32),
            jax.ShapeDtypeStruct((T, D_MODEL), jnp.bfloat16),
            jax.ShapeDtypeStruct((T, LANES), jnp.int32),
            jax.ShapeDtypeStruct((T, LANES), jnp.float32),
            jax.ShapeDtypeStruct((T, LANES), jnp.int32),
            jax.ShapeDtypeStruct((T // tm, 1, LANES), jnp.int32),
        ],
        compiler_params=_cparams(("parallel",)),
        name="merge_router",
    )(attn2, rec2, proj2, proj2, x2, wa, wr, wo, gba, gbr, n2, rw, rb)


SEG = 16
SEG_PIECES = 6
FILL_GROUPS = 1 << (SEG_PIECES - 1)
TABLE_LEAD = 2
TABLE_TAIL = 1


def _segment_copies(src, src_off, dst, dst_off, n_seg, sem, wait):
    for k in range(SEG_PIECES):
        @pl.when(((n_seg >> k) & 1) == 1)
        def _(k=k):
            off = n_seg & ((1 << k) - 1)
            cp = pltpu.make_async_copy(src.at[pl.ds(src_off + off, 1 << k)],
                                       dst.at[pl.ds(dst_off + off, 1 << k)], sem)
            if wait:
                cp.wait()
            else:
                cp.start()


def _tile_copies(local, hbm, to_hbm, row, nseg_ref, lss_ref, gseg_ref, sem, wait):
    def per_expert(e, carry):
        k = row * N_EXPERTS + e
        (src, src_off), (dst, dst_off) = (local, lss_ref[k]), (hbm, gseg_ref[k])
        if not to_hbm:
            (src, src_off), (dst, dst_off) = (dst, dst_off), (src, src_off)
        _segment_copies(src, src_off, dst, dst_off, nseg_ref[k], sem, wait)
        return carry

    lax.fori_loop(0, N_EXPERTS, per_expert, 0)


def _local_positions(idx_ref, rank_ref, lss_ref):
    tm = idx_ref.shape[0]
    lane = lax.broadcasted_iota(jnp.int32, (tm, LANES), 1)
    idx = idx_ref[...]
    lss = lss_ref[...].astype(jnp.float32)
    pos = jnp.zeros((tm, LANES), jnp.float32)
    for j in range(TOP_K):
        start = jnp.sum(jnp.where(lane == idx[:, j:j + 1], lss, 0.0), axis=-1, keepdims=True)
        pos = jnp.where(lane == j, start, pos)
    return pos + rank_ref[...].astype(jnp.float32)


def _dispatch_kernel(nseg_ref, lss_s_ref, gseg_ref, fill_ref, xn_ref, idx_ref, rank_ref, lss_ref,
                     zero_ref, xs_ref, buf_ref, sem, *, rc):
    i = pl.program_id(0)
    tm = xn_ref.shape[0]
    r_loc = buf_ref.shape[1] * SEG

    @pl.when(i == 0)
    def _():
        fill_sem = sem.at[0]

        def fill_range(e, carry):
            lo = fill_ref[0, e]
            n = fill_ref[1, e] - lo
            n_big = n // FILL_GROUPS
            rem = n - n_big * FILL_GROUPS

            def big(b, wait):
                cp = pltpu.make_async_copy(
                    zero_ref, xs_ref.at[pl.ds(lo + b * FILL_GROUPS, FILL_GROUPS)], fill_sem)
                if wait:
                    cp.wait()
                else:
                    cp.start()

            lax.fori_loop(0, n_big, lambda b, c: (big(b, False), c)[1], 0)
            _segment_copies(zero_ref, 0, xs_ref, lo + n_big * FILL_GROUPS, rem, fill_sem, False)
            lax.fori_loop(0, n_big, lambda b, c: (big(b, True), c)[1], 0)
            _segment_copies(zero_ref, 0, xs_ref, lo + n_big * FILL_GROUPS, rem, fill_sem, True)
            return carry

        lax.fori_loop(0, N_EXPERTS + 1, fill_range, 0)

    slot = i % 2
    _tile_copies(buf_ref.at[slot], xs_ref, True, i, nseg_ref, lss_s_ref, gseg_ref, sem.at[slot],
                 wait=True)

    pos_t = _local_positions(idx_ref, rank_ref, lss_ref).T[:TOP_K, :]
    x = xn_ref[...]
    for c in range(r_loc // rc):
        rho = (lax.broadcasted_iota(jnp.int32, (rc, tm), 0) + c * rc).astype(jnp.float32)
        hit = rho == pos_t[0:1, :]
        for j in range(1, TOP_K):
            hit = hit | (rho == pos_t[j:j + 1, :])
        perm = jnp.where(hit, 1.0, 0.0).astype(jnp.bfloat16)
        buf_ref[slot, pl.ds(c * rc // SEG, rc // SEG)] = jnp.dot(
            perm, x, preferred_element_type=jnp.float32).astype(buf_ref.dtype).reshape(
                rc // SEG, SEG, D_MODEL)

    _tile_copies(buf_ref.at[1 - slot], xs_ref, True, i + 1, nseg_ref, lss_s_ref, gseg_ref,
                 sem.at[1 - slot], wait=False)

    @pl.when(i == pl.num_programs(0) - 1)
    def _():
        _tile_copies(buf_ref.at[1 - slot], xs_ref, True, i + 1, nseg_ref, lss_s_ref, gseg_ref,
                     sem.at[1 - slot], wait=True)


def _dispatch(nseg, lss_flat, gseg, fill, xn2, idx, rank, lss3, n_rows, tm, r_loc):
    n_tiles = xn2.shape[0] // tm
    zeros = jnp.zeros((FILL_GROUPS, SEG, D_MODEL), xn2.dtype)
    smem = pl.BlockSpec(memory_space=pltpu.SMEM)
    tile = lambda i: jnp.minimum(i, n_tiles - 1)
    meta = pl.BlockSpec((tm, LANES), lambda i: (tile(i), 0))
    return pl.pallas_call(
        functools.partial(_dispatch_kernel, rc=MXU_DIM),
        grid=(n_tiles + 1,),
        in_specs=[smem, smem, smem, smem,
                  pl.BlockSpec((tm, D_MODEL), lambda i: (tile(i), 0)), meta, meta,
                  pl.BlockSpec((None, 1, LANES), lambda i: (tile(i), 0, 0)),
                  pl.BlockSpec((FILL_GROUPS, SEG, D_MODEL), lambda i: (0, 0, 0))],
        out_specs=pl.BlockSpec(memory_space=pl.ANY),
        out_shape=jax.ShapeDtypeStruct((n_rows // SEG, SEG, D_MODEL), xn2.dtype),
        scratch_shapes=[pltpu.VMEM((2, r_loc // SEG, SEG, D_MODEL), xn2.dtype),
                        pltpu.SemaphoreType.DMA((2,))],
        compiler_params=_cparams(("arbitrary",)),
        name="dispatch",
    )(nseg, lss_flat, gseg, fill, xn2, idx, rank, lss3, zeros)


def _moe_kernel(be_ref, nu_ref, xs_ref, w1_ref, b1_ref, w2_ref, b2_ref, ys_ref, w1b_ref, w2b_ref):
    i = pl.program_id(0)
    prev = be_ref[jnp.maximum(i - 1, 0)]

    @pl.when((i == 0) | (be_ref[i] != prev))
    def _():
        w1b_ref[...] = w1_ref[...].astype(jnp.bfloat16)
        w2b_ref[...] = w2_ref[...].astype(jnp.bfloat16)

    @pl.when(i < nu_ref[0])
    def _():
        h = jnp.dot(xs_ref[...], w1b_ref[...],
                    preferred_element_type=jnp.float32) + b1_ref[...]
        d = h.shape[1] // 2
        glu = jnp.minimum(h[:, :d], SWIGLU_LIMIT)
        lin = jnp.clip(h[:, d:], -SWIGLU_LIMIT, SWIGLU_LIMIT)
        act = glu * _sigmoid(SWIGLU_ALPHA * glu) * (lin + 1.0)
        y = jnp.dot(act.astype(jnp.bfloat16), w2b_ref[...],
                    preferred_element_type=jnp.float32) + b2_ref[...]
        ys_ref[...] = y.astype(ys_ref.dtype)

    @pl.when(i >= nu_ref[0])
    def _():
        ys_ref[...] = jnp.zeros_like(ys_ref)


def _moe(block_expert, n_used, xs, w1, b1, w2, b2, te):
    n_blocks = xs.shape[0] // te
    d_ff2 = w1.shape[2]
    grid_spec = pltpu.PrefetchScalarGridSpec(
        num_scalar_prefetch=2,
        grid=(n_blocks,),
        in_specs=[
            pl.BlockSpec((te, D_MODEL), lambda i, be, nu: (jnp.clip(i, 0, jnp.maximum(nu[0] - 1, 0)), 0)),
            pl.BlockSpec((None, D_MODEL, d_ff2), lambda i, be, nu: (be[i], 0, 0)),
            pl.BlockSpec((None, 1, d_ff2), lambda i, be, nu: (be[i], 0, 0)),
            pl.BlockSpec((None, d_ff2 // 2, D_MODEL), lambda i, be, nu: (be[i], 0, 0)),
            pl.BlockSpec((None, 1, D_MODEL), lambda i, be, nu: (be[i], 0, 0)),
        ],
        out_specs=pl.BlockSpec((te, D_MODEL), lambda i, be, nu: (i, 0)),
        scratch_shapes=[pltpu.VMEM((D_MODEL, d_ff2), jnp.bfloat16),
                        pltpu.VMEM((d_ff2 // 2, D_MODEL), jnp.bfloat16)],
    )
    return pl.pallas_call(
        _moe_kernel,
        grid_spec=grid_spec,
        out_shape=jax.ShapeDtypeStruct(xs.shape, xs.dtype),
        compiler_params=_cparams(("arbitrary",)),
        name="moe_experts",
    )(block_expert, n_used, xs, w1, b1, w2, b2)


def _combine_kernel(nseg_ref, lss_s_ref, gseg_ref, idx_ref, rank_ref, lss_ref, gate_ref, x1_ref,
                    fs_ref, ys_ref, o_ref, buf_ref, sem, *, rc):
    i = pl.program_id(0)
    tm = x1_ref.shape[0]
    r_loc = buf_ref.shape[1] * SEG
    slot = i % 2
    row = i + TABLE_LEAD

    @pl.when(i == 0)
    def _():
        buf_ref[...] = jnp.zeros_like(buf_ref)
        _tile_copies(buf_ref.at[0], ys_ref, False, TABLE_LEAD, nseg_ref, lss_s_ref, gseg_ref,
                     sem.at[0], wait=False)

    _tile_copies(buf_ref.at[1 - slot], ys_ref, False, row + 1, nseg_ref, lss_s_ref, gseg_ref,
                 sem.at[1 - slot], wait=False)
    _tile_copies(buf_ref.at[slot], ys_ref, False, row, nseg_ref, lss_s_ref, gseg_ref,
                 sem.at[slot], wait=True)

    pos = _local_positions(idx_ref, rank_ref, lss_ref)
    gates = gate_ref[...]
    acc = x1_ref[...]
    for c in range(r_loc // rc):
        rho = (lax.broadcasted_iota(jnp.int32, (tm, rc), 1) + c * rc).astype(jnp.float32)
        g = jnp.zeros((tm, rc), jnp.float32)
        for j in range(TOP_K):
            g = g + jnp.where(rho == pos[:, j:j + 1], gates[:, j:j + 1], 0.0)
        rows = buf_ref[slot, pl.ds(c * rc // SEG, rc // SEG)].reshape(rc, D_MODEL)
        acc = acc + jnp.dot(g.astype(jnp.bfloat16), rows, preferred_element_type=jnp.float32)
    o_ref[...] = _rms(acc, fs_ref[...])


def _combine(nseg, lss_flat, gseg, idx, rank, lss3, gates, x1, fscale, ys, tm, r_loc):
    T = x1.shape[0]
    smem = pl.BlockSpec(memory_space=pltpu.SMEM)
    meta = pl.BlockSpec((tm, LANES), lambda i: (i, 0))
    return pl.pallas_call(
        functools.partial(_combine_kernel, rc=MXU_DIM),
        grid=(T // tm,),
        in_specs=[smem, smem, smem, meta, meta,
                  pl.BlockSpec((None, 1, LANES), lambda i: (i, 0, 0)), meta,
                  pl.BlockSpec((tm, D_MODEL), lambda i: (i, 0)),
                  pl.BlockSpec((1, D_MODEL), lambda i: (0, 0)),
                  pl.BlockSpec(memory_space=pl.ANY)],
        out_specs=pl.BlockSpec((tm, D_MODEL), lambda i: (i, 0)),
        out_shape=jax.ShapeDtypeStruct((T, D_MODEL), jnp.float32),
        scratch_shapes=[pltpu.VMEM((2, r_loc // SEG, SEG, D_MODEL), ys.dtype),
                        pltpu.SemaphoreType.DMA((2,))],
        compiler_params=_cparams(("arbitrary",)),
        name="combine",
    )(nseg, lss_flat, gseg, idx, rank, lss3, gates, x1, fscale, ys)


def _block_diag_groups(wa, wx):
    per = LRU_GROUP // LRU_BLOCK

    def bd(w):
        w = w.reshape(N_LRU_GROUPS, per, LRU_BLOCK, LRU_BLOCK)
        eye = jnp.eye(per, dtype=w.dtype)
        return jnp.einsum('gnij,nm->gnimj', w, eye).reshape(N_LRU_GROUPS, LRU_GROUP, LRU_GROUP)

    return jnp.concatenate([bd(wa), bd(wx)], axis=-1).astype(jnp.bfloat16)


def _pick_tile(n, pref):
    t = min(pref, n)
    while n % t:
        t //= 2
    return t


def _layer(x, norm1_scale, w_in, f_bias, gate_bias, conv_w, conv_b, rg_wa, rg_ba, rg_wx, rg_bx,
           rg_lambda, w_attn_proj, w_rec_proj, w_out, norm2_scale, router_w, router_b,
           moe_w1, moe_b1, moe_w2, moe_b2, final_scale):
    B, S, D = x.shape
    T = B * S
    A = T * TOP_K
    x2 = x.reshape(T, D)
    row = lambda v: v.reshape(1, -1).astype(jnp.float32)

    aw = N_HEADS * HEAD_DIM
    o_f = 3 * aw
    o_x = o_f + N_HEADS
    w_main = jnp.concatenate([w_in[:, :aw] * (HEAD_DIM ** -0.5 * LOG2E), w_in[:, aw:o_f], w_in[:, o_x:]],
                             axis=1).astype(jnp.bfloat16)
    w_f = jnp.pad(w_in[:, o_f:o_x], ((0, 0), (0, LANES - N_HEADS))).astype(jnp.bfloat16)
    f_b = jnp.pad(row(f_bias), ((0, 0), (0, LANES - N_HEADS)))

    proj, f_logit = _in_proj(x2, row(norm1_scale), w_main, w_f, _pick_tile(T, TILE_IN_PROJ))
    proj3 = proj.reshape(B, S, PROJ_COLS)

    c = _fgate(f_logit.reshape(B, S, LANES), f_b, chunk=_pick_tile(S, MXU_DIM))
    attn = _attention(proj3, c.reshape(B, N_SLABS, HEADS_PER_SLAB, S), _pick_tile(S, TILE_ATTN_Q),
                      _pick_tile(S, TILE_ATTN_K))

    rec = _rglru(proj3, conv_w.astype(jnp.float32), row(conv_b), _block_diag_groups(rg_wa, rg_wx),
                 row(rg_ba), row(rg_bx), row(rg_lambda), _pick_tile(S, TILE_SCAN))

    rw32 = jnp.pad(router_w.astype(jnp.float32), ((0, 0), (0, LANES - N_EXPERTS)))
    rw_hi = rw32.astype(jnp.bfloat16)
    rw = jnp.concatenate([rw_hi, (rw32 - rw_hi.astype(jnp.float32)).astype(jnp.bfloat16)], axis=1)
    rb = jnp.pad(row(router_b), ((0, 0), (0, LANES - N_EXPERTS)), constant_values=NEG_BIG)
    gb = row(gate_bias)
    tm = _pick_tile(T, TILE_TOKENS)
    x1, xn2, top_idx, gates, rank, tile_cnt = _merge(
        attn.reshape(T, D), rec.reshape(T, D), proj, x2,
        w_attn_proj.astype(jnp.bfloat16), w_rec_proj.astype(jnp.bfloat16), w_out.astype(jnp.bfloat16),
        gb[:, :D], gb[:, D:], row(norm2_scale), rw, rb, tm)

    n_tiles = T // tm
    te = TILE_EXPERT
    n_blocks = -(-(A + (SEG - 1) * N_EXPERTS * n_tiles) // te) + N_EXPERTS
    n_rows = n_blocks * te
    r_loc = -(-(TOP_K * tm + (SEG - 1) * N_EXPERTS) // MXU_DIM) * MXU_DIM
    cnt = tile_cnt[:, 0, :N_EXPERTS]
    seg = (cnt + SEG - 1) // SEG * SEG
    lss = jnp.cumsum(seg, axis=1) - seg
    total = jnp.sum(seg, axis=0)
    padded = (total + te - 1) // te * te
    pad_end = jnp.cumsum(padded)
    pad_start = pad_end - padded
    gseg = pad_start[None, :] + jnp.cumsum(seg, axis=0) - seg
    n_used = (pad_end[-1] // te).astype(jnp.int32)
    blk = jnp.arange(n_blocks, dtype=jnp.int32)
    be = jnp.sum(pad_end[None, :] <= (blk * te)[:, None], axis=-1)
    block_expert = jnp.minimum(jnp.minimum(be, be[jnp.maximum(n_used - 1, 0)]),
                               N_EXPERTS - 1).astype(jnp.int32)
    fill = jnp.stack([jnp.concatenate([pad_start + total, pad_end[-1:]]),
                      jnp.concatenate([pad_end, jnp.full((1,), n_rows, pad_end.dtype)])]).astype(jnp.int32)
    flat = lambda a: jnp.pad(a, ((TABLE_LEAD, TABLE_TAIL), (0, 0))).reshape(-1).astype(jnp.int32)
    lss3 = jnp.pad(lss, ((0, 0), (0, LANES - N_EXPERTS))).reshape(n_tiles, 1, LANES).astype(jnp.int32)

    xs = _dispatch(flat(seg // SEG), flat(lss // SEG), flat(gseg // SEG), fill // SEG, xn2, top_idx,
                   rank, lss3, n_rows, tm, r_loc)
    ys = _moe(block_expert, n_used.reshape(1), xs.reshape(n_rows, D), moe_w1, moe_b1.reshape(N_EXPERTS, 1, -1),
              moe_w2, moe_b2.reshape(N_EXPERTS, 1, -1), te)
    out = _combine(flat(seg // SEG), flat(lss // SEG), flat(gseg // SEG), top_idx, rank, lss3, gates, x1,
                   row(final_scale), ys.reshape(n_rows // SEG, SEG, D), tm, r_loc)
    return out.reshape(B, S, D)


def kernel(x, norm1_scale, w_in, f_bias, gate_bias, conv_w, conv_b, rg_wa, rg_ba, rg_wx, rg_bx, rg_lambda, w_attn_proj, w_rec_proj, w_out, norm2_scale, router_w, router_b, moe_w1, moe_b1, moe_w2, moe_b2, final_norm_scale):
    depth = norm1_scale.shape[0]
    assert depth == 1, "the final rmsnorm is fused into the single layer's combine step"
    l = 0
    return _layer(x, norm1_scale[l], w_in[l], f_bias[l], gate_bias[l], conv_w[l], conv_b[l],
                  rg_wa[l], rg_ba[l], rg_wx[l], rg_bx[l], rg_lambda[l], w_attn_proj[l],
                  w_rec_proj[l], w_out[l], norm2_scale[l], router_w[l], router_b[l],
                  moe_w1[l], moe_b1[l], moe_w2[l], moe_b2[l], final_norm_scale)
```

```python
import functools

import jax
import jax.numpy as jnp
from jax import lax
from jax.experimental import pallas as pl
from jax.experimental.pallas import tpu as pltpu

D_MODEL = 1024
N_HEADS = 16
HEAD_DIM = 64
LANES = 128
MXU_DIM = 256
HEADS_PER_SLAB = LANES // HEAD_DIM
N_SLABS = D_MODEL // LANES
N_LRU_BLOCKS = 16
LRU_BLOCK = 64
LRU_GROUP = MXU_DIM
N_LRU_GROUPS = D_MODEL // LRU_GROUP
CONV_W = 4
LRU_C = 8.0
N_EXPERTS = 32
TOP_K = 4
SWIGLU_ALPHA = 1.702
SWIGLU_LIMIT = 7.0
RMS_EPS = 1e-6
NEG_BIG = -1e30
LOG2E = 1.4426950408889634

PROJ_COLS = 7 * D_MODEL
COL_Q, COL_K, COL_V, COL_XREC, COL_YREC, COL_GA, COL_GR = range(7)

VMEM_LIMIT = 56 * 1024 * 1024

TILE_IN_PROJ = 2048
TILE_ATTN_Q = 256
TILE_ATTN_K = MXU_DIM
TILE_SCAN = 512
TILE_TOKENS = 512
TILE_EXPERT = 512


def _cparams(sem):
    return pltpu.CompilerParams(dimension_semantics=sem, vmem_limit_bytes=VMEM_LIMIT)


def _softplus(z):
    return jnp.maximum(z, 0.0) + jnp.log1p(jnp.exp(-jnp.abs(z)))


def _sigmoid(z):
    return 0.5 * jnp.tanh(0.5 * z) + 0.5


def _rms(x, scale):
    return x * lax.rsqrt(jnp.mean(x * x, axis=-1, keepdims=True) + RMS_EPS) * scale


def _in_proj_kernel(x_ref, scale_ref, w_ref, wf_ref, proj_ref, f_ref, xn_ref):
    @pl.when(pl.program_id(1) == 0)
    def _():
        xn = _rms(x_ref[...], scale_ref[...]).astype(jnp.bfloat16)
        xn_ref[...] = xn
        f_ref[...] = jnp.dot(xn, wf_ref[...], preferred_element_type=jnp.float32)

    proj_ref[...] = jnp.dot(xn_ref[...], w_ref[...],
                            preferred_element_type=jnp.float32).astype(proj_ref.dtype)


def _in_proj(x2, scale, w, wf, tm):
    T = x2.shape[0]
    n_col = w.shape[1] // D_MODEL
    return pl.pallas_call(
        _in_proj_kernel,
        grid=(T // tm, n_col),
        in_specs=[
            pl.BlockSpec((tm, D_MODEL), lambda i, j: (i, 0)),
            pl.BlockSpec((1, D_MODEL), lambda i, j: (0, 0)),
            pl.BlockSpec((D_MODEL, D_MODEL), lambda i, j: (0, j)),
            pl.BlockSpec((D_MODEL, LANES), lambda i, j: (0, 0)),
        ],
        out_specs=[
            pl.BlockSpec((tm, D_MODEL), lambda i, j: (i, j)),
            pl.BlockSpec((tm, LANES), lambda i, j: (i, 0)),
        ],
        out_shape=[
            jax.ShapeDtypeStruct((T, w.shape[1]), jnp.bfloat16),
            jax.ShapeDtypeStruct((T, LANES), jnp.float32),
        ],
        scratch_shapes=[pltpu.VMEM((tm, D_MODEL), jnp.bfloat16)],
        compiler_params=_cparams(("parallel", "arbitrary")),
        name="in_proj",
    )(x2, scale, w, wf)


def _fgate_kernel(f_ref, fb_ref, c_ref, *, chunk):
    S = f_ref.shape[0]
    row = lax.broadcasted_iota(jnp.int32, (chunk, chunk), 0)
    col = lax.broadcasted_iota(jnp.int32, (chunk, chunk), 1)
    tri = (row >= col).astype(jnp.float32)
    carry = jnp.zeros((1, LANES), jnp.float32)
    for c in range(S // chunk):
        z = f_ref[pl.ds(c * chunk, chunk), :] + fb_ref[...]
        log_f = -_softplus(-z)
        cs = jnp.dot(tri, log_f, precision=lax.Precision.HIGHEST,
                     preferred_element_type=jnp.float32) + carry
        carry = cs[chunk - 1:chunk, :]
        c_ref[:, pl.ds(c * chunk, chunk)] = (cs * LOG2E).T[:N_HEADS, :]


def _fgate(f3, fb, chunk):
    B, S, _ = f3.shape
    return pl.pallas_call(
        functools.partial(_fgate_kernel, chunk=chunk),
        grid=(B,),
        in_specs=[
            pl.BlockSpec((None, S, LANES), lambda b: (b, 0, 0)),
            pl.BlockSpec((1, LANES), lambda b: (0, 0)),
        ],
        out_specs=pl.BlockSpec((None, N_HEADS, S), lambda b: (b, 0, 0)),
        out_shape=jax.ShapeDtypeStruct((B, N_HEADS, S), jnp.float32),
        compiler_params=_cparams(("parallel",)),
        name="fgate",
    )(f3, fb)


def _attn_kernel(q_ref, k_ref, v_ref, c_ref, o_ref, *, tq, tk):
    S = q_ref.shape[0]
    lane = lax.broadcasted_iota(jnp.int32, (1, LANES), 1)
    head0 = lane < HEAD_DIM
    row = lax.broadcasted_iota(jnp.int32, (HEADS_PER_SLAB * tq, tk), 0) % tq
    col = lax.broadcasted_iota(jnp.int32, (HEADS_PER_SLAB * tq, tk), 1)

    def chunk(ks, carry, q2, mask):
        m, l, acc = carry
        s = lax.dot_general(q2, k_ref[pl.ds(ks, tk), :], (((1,), (1,)), ((), ())),
                            preferred_element_type=jnp.float32)
        cb = c_ref[:, pl.ds(ks, tk)]
        s = jnp.concatenate([s[:tq] - cb[0:1], s[tq:] - cb[1:2]], axis=0)
        if mask is not None:
            s = jnp.where(mask, s, NEG_BIG)
        m_new = jnp.maximum(m, jnp.max(s, axis=-1, keepdims=True))
        alpha = jnp.exp2(m - m_new)
        p = jnp.exp2(s - m_new)
        l = alpha * l + sum(p[:, j * LANES:(j + 1) * LANES] for j in range(tk // LANES))
        pv = jnp.dot(p.astype(jnp.bfloat16), v_ref[pl.ds(ks, tk), :],
                     preferred_element_type=jnp.float32)
        return m_new, l, alpha * acc + pv

    for qi in range(S // tq):
        qs = qi * tq
        q = q_ref[pl.ds(qs, tq), :]
        zero = jnp.zeros_like(q)
        q2 = jnp.concatenate([jnp.where(head0, q, zero), jnp.where(head0, zero, q)], axis=0)
        carry = (jnp.full((HEADS_PER_SLAB * tq, 1), NEG_BIG, jnp.float32),
                 jnp.zeros((HEADS_PER_SLAB * tq, LANES), jnp.float32),
                 jnp.zeros((HEADS_PER_SLAB * tq, LANES), jnp.float32))
        n_full = qs // tk
        n_chunks = -(-(qs + tq) // tk)
        for kc in range(n_chunks):
            mask = None if kc < n_full else (row + qs) >= (col + kc * tk)
            carry = chunk(kc * tk, carry, q2, mask)
        _, l, acc = carry
        out = acc / jnp.sum(l, axis=-1, keepdims=True)
        o_ref[pl.ds(qs, tq), :] = jnp.where(head0, out[:tq], out[tq:]).astype(o_ref.dtype)


def _attention(proj3, c4, tq, tk):
    B, S, _ = proj3.shape
    slab = lambda c: pl.BlockSpec((None, S, LANES), lambda b, p: (b, 0, c * N_SLABS + p))
    return pl.pallas_call(
        functools.partial(_attn_kernel, tq=tq, tk=tk),
        grid=(B, N_SLABS),
        in_specs=[slab(COL_Q), slab(COL_K), slab(COL_V),
                  pl.BlockSpec((None, None, HEADS_PER_SLAB, S), lambda b, p: (b, p, 0, 0))],
        out_specs=pl.BlockSpec((None, S, LANES), lambda b, p: (b, 0, p)),
        out_shape=jax.ShapeDtypeStruct((B, S, D_MODEL), jnp.bfloat16),
        compiler_params=_cparams(("parallel", "parallel")),
        name="attn",
    )(proj3, proj3, proj3, c4)


def _rglru_kernel(x_ref, y_ref, cw_ref, cb_ref, wg_ref, ba_ref, bx_ref, lam_ref, o_ref,
                  xe_ref, a_ref, b_ref, h_ref, *, ts):
    HALO = 8
    s_idx = pl.program_id(1)

    @pl.when(s_idx == 0)
    def _():
        xe_ref[pl.ds(0, HALO), :] = jnp.zeros((HALO, D_MODEL), jnp.float32)
        h_ref[...] = jnp.zeros_like(h_ref)

    xe_ref[pl.ds(HALO, ts), :] = x_ref[...].astype(jnp.float32)
    conv = cb_ref[...] + cw_ref[CONV_W - 1:CONV_W, :] * xe_ref[pl.ds(HALO, ts), :]
    for j in range(CONV_W - 1):
        shift = CONV_W - 1 - j
        conv = conv + cw_ref[j:j + 1, :] * xe_ref[pl.ds(HALO - shift, ts), :]
    xe_ref[pl.ds(0, HALO), :] = xe_ref[pl.ds(ts, HALO), :]

    cb16 = conv.astype(jnp.bfloat16)
    sp = _softplus(-lam_ref[...])
    first = (lax.broadcasted_iota(jnp.int32, (ts, 1), 0) + s_idx * ts) == 0
    for g in range(N_LRU_GROUPS):
        cols = slice(g * LRU_GROUP, (g + 1) * LRU_GROUP)
        gates = jnp.dot(cb16[:, cols], wg_ref[g], preferred_element_type=jnp.float32)
        r = _sigmoid(gates[:, :LRU_GROUP] + ba_ref[:, cols])
        i = _sigmoid(gates[:, LRU_GROUP:] + bx_ref[:, cols])
        log_a = -LRU_C * r * sp[:, cols]
        a = jnp.exp(log_a)
        mult = jnp.where(first, 1.0, jnp.sqrt(1.0 - a * a))
        a_ref[:, cols] = a
        b_ref[:, cols] = mult * (i * conv[:, cols])

    SUB = 8
    rows = lax.broadcasted_iota(jnp.int32, (SUB, D_MODEL), 0)

    def slab(i, h_prev):
        start = pl.multiple_of(i * SUB, SUB)
        a = a_ref[pl.ds(start, SUB), :]
        b = b_ref[pl.ds(start, SUB), :]
        for d in (1, 2, 4):
            keep = rows >= d
            a_s = jnp.where(keep, pltpu.roll(a, d, 0), 1.0)
            b_s = jnp.where(keep, pltpu.roll(b, d, 0), 0.0)
            b = a * b_s + b
            a = a * a_s
        h = a * h_prev + b
        gate = jax.nn.gelu(y_ref[pl.ds(start, SUB), :].astype(jnp.float32))
        o_ref[pl.ds(start, SUB), :] = (h * gate).astype(o_ref.dtype)
        return jnp.broadcast_to(h[SUB - 1:SUB, :], (SUB, D_MODEL))

    h_ref[...] = lax.fori_loop(0, ts // SUB, slab, h_ref[...], unroll=4)


def _rglru(proj3, conv_w, conv_b, wg, ba, bx, lam, ts):
    B, S, _ = proj3.shape
    full = lambda shape: pl.BlockSpec(shape, lambda b, s: (0,) * len(shape))
    return pl.pallas_call(
        functools.partial(_rglru_kernel, ts=ts),
        grid=(B, S // ts),
        in_specs=[
            pl.BlockSpec((None, ts, D_MODEL), lambda b, s: (b, s, COL_XREC)),
            pl.BlockSpec((None, ts, D_MODEL), lambda b, s: (b, s, COL_YREC)),
            full((CONV_W, D_MODEL)),
            full((1, D_MODEL)),
            full((N_LRU_GROUPS, LRU_GROUP, 2 * LRU_GROUP)),
            full((1, D_MODEL)),
            full((1, D_MODEL)),
            full((1, D_MODEL)),
        ],
        out_specs=pl.BlockSpec((None, ts, D_MODEL), lambda b, s: (b, s, 0)),
        out_shape=jax.ShapeDtypeStruct((B, S, D_MODEL), jnp.bfloat16),
        scratch_shapes=[
            pltpu.VMEM((ts + 8, D_MODEL), jnp.float32),
            pltpu.VMEM((ts, D_MODEL), jnp.float32),
            pltpu.VMEM((ts, D_MODEL), jnp.float32),
            pltpu.VMEM((8, D_MODEL), jnp.float32),
        ],
        compiler_params=_cparams(("parallel", "arbitrary")),
        name="rglru",
    )(proj3, proj3, conv_w, conv_b, wg, ba, bx, lam)


def _merge_kernel(attn_ref, rec_ref, ga_ref, gr_ref, x_ref, wa_ref, wr_ref, wo_ref,
                  gba_ref, gbr_ref, n2_ref, rw_ref, rb_ref,
                  x1_ref, xn2_ref, idx_ref, gate_ref, rank_ref, cnt_ref, *, tm):
    pa = jnp.dot(attn_ref[...], wa_ref[...], preferred_element_type=jnp.float32)
    pr = jnp.dot(rec_ref[...], wr_ref[...], preferred_element_type=jnp.float32)
    g_a = _sigmoid(ga_ref[...].astype(jnp.float32) + gba_ref[...])
    g_r = _sigmoid(gr_ref[...].astype(jnp.float32) + gbr_ref[...])
    merged = (g_a * pa + g_r * pr).astype(jnp.bfloat16)
    x1 = x_ref[...] + jnp.dot(merged, wo_ref[...], preferred_element_type=jnp.float32)
    x1_ref[...] = x1
    xn2 = _rms(x1, n2_ref[...])
    xh = xn2.astype(jnp.bfloat16)
    xn2_ref[...] = xh

    xl = (xn2 - xh.astype(jnp.float32)).astype(jnp.bfloat16)
    part = jnp.dot(xh, rw_ref[...], preferred_element_type=jnp.float32)
    logits = (part[:, :LANES] + part[:, LANES:]
              + jnp.dot(xl, rw_ref[:, :LANES], preferred_element_type=jnp.float32)) + rb_ref[...]
    lane = lax.broadcasted_iota(jnp.int32, (tm, LANES), 1).astype(jnp.float32)
    idx_out = jnp.zeros((tm, LANES), jnp.float32)
    val_out = jnp.zeros((tm, LANES), jnp.float32)
    onehot = jnp.zeros((tm, LANES), jnp.float32)
    sel = []
    work = logits
    for j in range(TOP_K):
        m = jnp.max(work, axis=-1, keepdims=True)
        idx = jnp.min(jnp.where(work == m, lane, float(LANES)), axis=-1, keepdims=True)
        hit = lane == idx
        sel.append(hit)
        onehot = jnp.where(hit, 1.0, onehot)
        idx_out = jnp.where(lane == float(j), idx, idx_out)
        val_out = jnp.where(lane == float(j), m, val_out)
        work = jnp.where(hit, NEG_BIG, work)
    is_slot = lane < float(TOP_K)
    e = jnp.where(is_slot, jnp.exp(val_out - jnp.max(jnp.where(is_slot, val_out, NEG_BIG),
                                                      axis=-1, keepdims=True)), 0.0)
    gate_ref[...] = e / jnp.sum(e, axis=-1, keepdims=True)
    idx_ref[...] = idx_out.astype(jnp.int32)

    row = lax.broadcasted_iota(jnp.int32, (tm, tm), 0)
    col = lax.broadcasted_iota(jnp.int32, (tm, tm), 1)
    strict = (row > col).astype(jnp.bfloat16)
    before = jnp.dot(strict, onehot.astype(jnp.bfloat16), preferred_element_type=jnp.float32)
    rank_out = jnp.zeros((tm, LANES), jnp.float32)
    for j in range(TOP_K):
        rj = jnp.sum(jnp.where(sel[j], before, 0.0), axis=-1, keepdims=True)
        rank_out = jnp.where(lane == float(j), rj, rank_out)
    rank_ref[...] = rank_out.astype(jnp.int32)
    cnt_ref[...] = jnp.sum(onehot, axis=0, keepdims=True).astype(jnp.int32)


def _merge(attn2, rec2, proj2, x2, wa, wr, wo, gba, gbr, n2, rw, rb, tm):
    T = x2.shape[0]
    tok = lambda c: pl.BlockSpec((tm, D_MODEL), lambda i: (i, c))
    full = lambda shape: pl.BlockSpec(shape, lambda i: (0,) * len(shape))
    meta = pl.BlockSpec((tm, LANES), lambda i: (i, 0))
    return pl.pallas_call(
        functools.partial(_merge_kernel, tm=tm),
        grid=(T // tm,),
        in_specs=[tok(0), tok(0), tok(COL_GA), tok(COL_GR), tok(0),
                  full((D_MODEL, D_MODEL)), full((D_MODEL, D_MODEL)), full((D_MODEL, D_MODEL)),
                  full((1, D_MODEL)), full((1, D_MODEL)), full((1, D_MODEL)),
                  full((D_MODEL, 2 * LANES)), full((1, LANES))],
        out_specs=[tok(0), tok(0), meta, meta, meta,
                   pl.BlockSpec((None, 1, LANES), lambda i: (i, 0, 0))],
        out_shape=[
            jax.ShapeDtypeStruct((T, D_MODEL), jnp.float32),
            jax.ShapeDtypeStruct((T, D_MODEL), jnp.bfloat16),
            jax.ShapeDtypeStruct((T, LANES), jnp.int32),
            jax.ShapeDtypeStruct((T, LANES), jnp.float32),
            jax.ShapeDtypeStruct((T, LANES), jnp.int32),
            jax.ShapeDtypeStruct((T // tm, 1, LANES), jnp.int32),
        ],
        compiler_params=_cparams(("parallel",)),
        name="merge_router",
    )(attn2, rec2, proj2, proj2, x2, wa, wr, wo, gba, gbr, n2, rw, rb)


SEG = 16
SEG_PIECES = 6
FILL_GROUPS = 1 << (SEG_PIECES - 1)
TABLE_LEAD = 2
TABLE_TAIL = 1


def _segment_copies(src, src_off, dst, dst_off, n_seg, sem, wait):
    for k in range(SEG_PIECES):
        @pl.when(((n_seg >> k) & 1) == 1)
        def _(k=k):
            off = n_seg & ((1 << k) - 1)
            cp = pltpu.make_async_copy(src.at[pl.ds(src_off + off, 1 << k)],
                                       dst.at[pl.ds(dst_off + off, 1 << k)], sem)
            if wait:
                cp.wait()
            else:
                cp.start(priority=k % 2)


def _tile_copies(local, hbm, to_hbm, row, nseg_ref, lss_ref, gseg_ref, sem, wait):
    def per_expert(e, carry):
        k = row * N_EXPERTS + e
        (src, src_off), (dst, dst_off) = (local, lss_ref[k]), (hbm, gseg_ref[k])
        if not to_hbm:
            (src, src_off), (dst, dst_off) = (dst, dst_off), (src, src_off)
        _segment_copies(src, src_off, dst, dst_off, nseg_ref[k], sem, wait)
        return carry

    lax.fori_loop(0, N_EXPERTS, per_expert, 0)


def _local_positions(idx_ref, rank_ref, lss_ref):
    tm = idx_ref.shape[0]
    lane = lax.broadcasted_iota(jnp.int32, (tm, LANES), 1)
    idx = idx_ref[...]
    lss = lss_ref[...].astype(jnp.float32)
    pos = jnp.zeros((tm, LANES), jnp.float32)
    for j in range(TOP_K):
        start = jnp.sum(jnp.where(lane == idx[:, j:j + 1], lss, 0.0), axis=-1, keepdims=True)
        pos = jnp.where(lane == j, start, pos)
    return pos + rank_ref[...].astype(jnp.float32)


def _dispatch_kernel(nseg_ref, lss_s_ref, gseg_ref, fill_ref, xn_ref, idx_ref, rank_ref, lss_ref,
                     zero_ref, xs_ref, buf_ref, sem, *, rc):
    i = pl.program_id(0)
    tm = xn_ref.shape[0]
    r_loc = buf_ref.shape[1] * SEG

    @pl.when(i == 0)
    def _():
        fill_sem = sem.at[0]

        def fill_range(e, carry):
            lo = fill_ref[0, e]
            n = fill_ref[1, e] - lo
            n_big = n // FILL_GROUPS
            rem = n - n_big * FILL_GROUPS

            def big(b, wait):
                cp = pltpu.make_async_copy(
                    zero_ref, xs_ref.at[pl.ds(lo + b * FILL_GROUPS, FILL_GROUPS)], fill_sem)
                if wait:
                    cp.wait()
                else:
                    cp.start()

            lax.fori_loop(0, n_big, lambda b, c: (big(b, False), c)[1], 0)
            _segment_copies(zero_ref, 0, xs_ref, lo + n_big * FILL_GROUPS, rem, fill_sem, False)
            lax.fori_loop(0, n_big, lambda b, c: (big(b, True), c)[1], 0)
            _segment_copies(zero_ref, 0, xs_ref, lo + n_big * FILL_GROUPS, rem, fill_sem, True)
            return carry

        lax.fori_loop(0, N_EXPERTS + 1, fill_range, 0)

    slot = i % 2
    _tile_copies(buf_ref.at[slot], xs_ref, True, i, nseg_ref, lss_s_ref, gseg_ref, sem.at[slot],
                 wait=True)

    pos_t = _local_positions(idx_ref, rank_ref, lss_ref).T[:TOP_K, :]
    x = xn_ref[...]
    for c in range(r_loc // rc):
        rho = (lax.broadcasted_iota(jnp.int32, (rc, tm), 0) + c * rc).astype(jnp.float32)
        hit = rho == pos_t[0:1, :]
        for j in range(1, TOP_K):
            hit = hit | (rho == pos_t[j:j + 1, :])
        perm = jnp.where(hit, 1.0, 0.0).astype(jnp.bfloat16)
        buf_ref[slot, pl.ds(c * rc // SEG, rc // SEG)] = jnp.dot(
            perm, x, preferred_element_type=jnp.float32).astype(buf_ref.dtype).reshape(
                rc // SEG, SEG, D_MODEL)

    _tile_copies(buf_ref.at[1 - slot], xs_ref, True, i + 1, nseg_ref, lss_s_ref, gseg_ref,
                 sem.at[1 - slot], wait=False)

    @pl.when(i == pl.num_programs(0) - 1)
    def _():
        _tile_copies(buf_ref.at[1 - slot], xs_ref, True, i + 1, nseg_ref, lss_s_ref, gseg_ref,
                     sem.at[1 - slot], wait=True)


def _dispatch(nseg, lss_flat, gseg, fill, xn2, idx, rank, lss3, n_rows, tm, r_loc):
    n_tiles = xn2.shape[0] // tm
    zeros = jnp.zeros((FILL_GROUPS, SEG, D_MODEL), xn2.dtype)
    smem = pl.BlockSpec(memory_space=pltpu.SMEM)
    tile = lambda i: jnp.minimum(i, n_tiles - 1)
    meta = pl.BlockSpec((tm, LANES), lambda i: (tile(i), 0))
    return pl.pallas_call(
        functools.partial(_dispatch_kernel, rc=MXU_DIM),
        grid=(n_tiles + 1,),
        in_specs=[smem, smem, smem, smem,
                  pl.BlockSpec((tm, D_MODEL), lambda i: (tile(i), 0)), meta, meta,
                  pl.BlockSpec((None, 1, LANES), lambda i: (tile(i), 0, 0)),
                  pl.BlockSpec((FILL_GROUPS, SEG, D_MODEL), lambda i: (0, 0, 0))],
        out_specs=pl.BlockSpec(memory_space=pl.ANY),
        out_shape=jax.ShapeDtypeStruct((n_rows // SEG, SEG, D_MODEL), xn2.dtype),
        scratch_shapes=[pltpu.VMEM((2, r_loc // SEG, SEG, D_MODEL), xn2.dtype),
                        pltpu.SemaphoreType.DMA((2,))],
        compiler_params=_cparams(("arbitrary",)),
        name="dispatch",
    )(nseg, lss_flat, gseg, fill, xn2, idx, rank, lss3, zeros)


def _moe_kernel(be_ref, nu_ref, xs_ref, w1_ref, b1_ref, w2_ref, b2_ref, ys_ref, w1b_ref, w2b_ref):
    i = pl.program_id(0)
    prev = be_ref[jnp.maximum(i - 1, 0)]

    @pl.when((i == 0) | (be_ref[i] != prev))
    def _():
        w1b_ref[...] = w1_ref[...].astype(jnp.bfloat16)
        w2b_ref[...] = w2_ref[...].astype(jnp.bfloat16)

    @pl.when(i < nu_ref[0])
    def _():
        h = jnp.dot(xs_ref[...], w1b_ref[...],
                    preferred_element_type=jnp.float32) + b1_ref[...]
        d = h.shape[1] // 2
        glu = jnp.minimum(h[:, :d], SWIGLU_LIMIT)
        lin = jnp.clip(h[:, d:], -SWIGLU_LIMIT, SWIGLU_LIMIT)
        act = glu * _sigmoid(SWIGLU_ALPHA * glu) * (lin + 1.0)
        y = jnp.dot(act.astype(jnp.bfloat16), w2b_ref[...],
                    preferred_element_type=jnp.float32) + b2_ref[...]
        ys_ref[...] = y.astype(ys_ref.dtype)

    @pl.when(i >= nu_ref[0])
    def _():
        ys_ref[...] = jnp.zeros_like(ys_ref)


def _moe(block_expert, n_used, xs, w1, b1, w2, b2, te):
    n_blocks = xs.shape[0] // te
    d_ff2 = w1.shape[2]
    grid_spec = pltpu.PrefetchScalarGridSpec(
        num_scalar_prefetch=2,
        grid=(n_blocks,),
        in_specs=[
            pl.BlockSpec((te, D_MODEL), lambda i, be, nu: (jnp.clip(i, 0, jnp.maximum(nu[0] - 1, 0)), 0)),
            pl.BlockSpec((None, D_MODEL, d_ff2), lambda i, be, nu: (be[i], 0, 0)),
            pl.BlockSpec((None, 1, d_ff2), lambda i, be, nu: (be[i], 0, 0)),
            pl.BlockSpec((None, d_ff2 // 2, D_MODEL), lambda i, be, nu: (be[i], 0, 0)),
            pl.BlockSpec((None, 1, D_MODEL), lambda i, be, nu: (be[i], 0, 0)),
        ],
        out_specs=pl.BlockSpec((te, D_MODEL), lambda i, be, nu: (i, 0)),
        scratch_shapes=[pltpu.VMEM((D_MODEL, d_ff2), jnp.bfloat16),
                        pltpu.VMEM((d_ff2 // 2, D_MODEL), jnp.bfloat16)],
    )
    return pl.pallas_call(
        _moe_kernel,
        grid_spec=grid_spec,
        out_shape=jax.ShapeDtypeStruct(xs.shape, xs.dtype),
        compiler_params=_cparams(("arbitrary",)),
        name="moe_experts",
    )(block_expert, n_used, xs, w1, b1, w2, b2)


def _combine_kernel(nseg_ref, lss_s_ref, gseg_ref, idx_ref, rank_ref, lss_ref, gate_ref, x1_ref,
                    fs_ref, ys_ref, o_ref, buf_ref, sem, *, rc):
    i = pl.program_id(0)
    tm = x1_ref.shape[0]
    r_loc = buf_ref.shape[1] * SEG
    slot = i % 2
    row = i + TABLE_LEAD

    @pl.when(i == 0)
    def _():
        buf_ref[...] = jnp.zeros_like(buf_ref)
        _tile_copies(buf_ref.at[0], ys_ref, False, TABLE_LEAD, nseg_ref, lss_s_ref, gseg_ref,
                     sem.at[0], wait=False)

    _tile_copies(buf_ref.at[1 - slot], ys_ref, False, row + 1, nseg_ref, lss_s_ref, gseg_ref,
                 sem.at[1 - slot], wait=False)
    _tile_copies(buf_ref.at[slot], ys_ref, False, row, nseg_ref, lss_s_ref, gseg_ref,
                 sem.at[slot], wait=True)

    pos = _local_positions(idx_ref, rank_ref, lss_ref)
    gates = gate_ref[...]
    acc = x1_ref[...]
    for c in range(r_loc // rc):
        rho = (lax.broadcasted_iota(jnp.int32, (tm, rc), 1) + c * rc).astype(jnp.float32)
        g = jnp.zeros((tm, rc), jnp.float32)
        for j in range(TOP_K):
            g = g + jnp.where(rho == pos[:, j:j + 1], gates[:, j:j + 1], 0.0)
        rows = buf_ref[slot, pl.ds(c * rc // SEG, rc // SEG)].reshape(rc, D_MODEL)
        acc = acc + jnp.dot(g.astype(jnp.bfloat16), rows, preferred_element_type=jnp.float32)
    o_ref[...] = _rms(acc, fs_ref[...])


def _combine(nseg, lss_flat, gseg, idx, rank, lss3, gates, x1, fscale, ys, tm, r_loc):
    T = x1.shape[0]
    smem = pl.BlockSpec(memory_space=pltpu.SMEM)
    meta = pl.BlockSpec((tm, LANES), lambda i: (i, 0))
    return pl.pallas_call(
        functools.partial(_combine_kernel, rc=MXU_DIM),
        grid=(T // tm,),
        in_specs=[smem, smem, smem, meta, meta,
                  pl.BlockSpec((None, 1, LANES), lambda i: (i, 0, 0)), meta,
                  pl.BlockSpec((tm, D_MODEL), lambda i: (i, 0)),
                  pl.BlockSpec((1, D_MODEL), lambda i: (0, 0)),
                  pl.BlockSpec(memory_space=pl.ANY)],
        out_specs=pl.BlockSpec((tm, D_MODEL), lambda i: (i, 0)),
        out_shape=jax.ShapeDtypeStruct((T, D_MODEL), jnp.float32),
        scratch_shapes=[pltpu.VMEM((2, r_loc // SEG, SEG, D_MODEL), ys.dtype),
                        pltpu.SemaphoreType.DMA((2,))],
        compiler_params=_cparams(("arbitrary",)),
        name="combine",
    )(nseg, lss_flat, gseg, idx, rank, lss3, gates, x1, fscale, ys)


def _block_diag_groups(wa, wx):
    per = LRU_GROUP // LRU_BLOCK

    def bd(w):
        w = w.reshape(N_LRU_GROUPS, per, LRU_BLOCK, LRU_BLOCK)
        eye = jnp.eye(per, dtype=w.dtype)
        return jnp.einsum('gnij,nm->gnimj', w, eye).reshape(N_LRU_GROUPS, LRU_GROUP, LRU_GROUP)

    return jnp.concatenate([bd(wa), bd(wx)], axis=-1).astype(jnp.bfloat16)


def _pick_tile(n, pref):
    t = min(pref, n)
    while n % t:
        t //= 2
    return t


def _layer(x, norm1_scale, w_in, f_bias, gate_bias, conv_w, conv_b, rg_wa, rg_ba, rg_wx, rg_bx,
           rg_lambda, w_attn_proj, w_rec_proj, w_out, norm2_scale, router_w, router_b,
           moe_w1, moe_b1, moe_w2, moe_b2, final_scale):
    B, S, D = x.shape
    T = B * S
    A = T * TOP_K
    x2 = x.reshape(T, D)
    row = lambda v: v.reshape(1, -1).astype(jnp.float32)

    aw = N_HEADS * HEAD_DIM
    o_f = 3 * aw
    o_x = o_f + N_HEADS
    w_main = jnp.concatenate([w_in[:, :aw] * (HEAD_DIM ** -0.5 * LOG2E), w_in[:, aw:o_f], w_in[:, o_x:]],
                             axis=1).astype(jnp.bfloat16)
    w_f = jnp.pad(w_in[:, o_f:o_x], ((0, 0), (0, LANES - N_HEADS))).astype(jnp.bfloat16)
    f_b = jnp.pad(row(f_bias), ((0, 0), (0, LANES - N_HEADS)))

    proj, f_logit = _in_proj(x2, row(norm1_scale), w_main, w_f, _pick_tile(T, TILE_IN_PROJ))
    proj3 = proj.reshape(B, S, PROJ_COLS)

    c = _fgate(f_logit.reshape(B, S, LANES), f_b, chunk=_pick_tile(S, MXU_DIM))
    attn = _attention(proj3, c.reshape(B, N_SLABS, HEADS_PER_SLAB, S), _pick_tile(S, TILE_ATTN_Q),
                      _pick_tile(S, TILE_ATTN_K))

    rec = _rglru(proj3, conv_w.astype(jnp.float32), row(conv_b), _block_diag_groups(rg_wa, rg_wx),
                 row(rg_ba), row(rg_bx), row(rg_lambda), _pick_tile(S, TILE_SCAN))

    rw32 = jnp.pad(router_w.astype(jnp.float32), ((0, 0), (0, LANES - N_EXPERTS)))
    rw_hi = rw32.astype(jnp.bfloat16)
    rw = jnp.concatenate([rw_hi, (rw32 - rw_hi.astype(jnp.float32)).astype(jnp.bfloat16)], axis=1)
    rb = jnp.pad(row(router_b), ((0, 0), (0, LANES - N_EXPERTS)), constant_values=NEG_BIG)
    gb = row(gate_bias)
    tm = _pick_tile(T, TILE_TOKENS)
    x1, xn2, top_idx, gates, rank, tile_cnt = _merge(
        attn.reshape(T, D), rec.reshape(T, D), proj, x2,
        w_attn_proj.astype(jnp.bfloat16), w_rec_proj.astype(jnp.bfloat16), w_out.astype(jnp.bfloat16),
        gb[:, :D], gb[:, D:], row(norm2_scale), rw, rb, tm)

    n_tiles = T // tm
    te = TILE_EXPERT
    n_blocks = -(-(A + (SEG - 1) * N_EXPERTS * n_tiles) // te) + N_EXPERTS
    n_rows = n_blocks * te
    r_loc = -(-(TOP_K * tm + (SEG - 1) * N_EXPERTS) // MXU_DIM) * MXU_DIM
    cnt = tile_cnt[:, 0, :N_EXPERTS]
    seg = (cnt + SEG - 1) // SEG * SEG
    lss = jnp.cumsum(seg, axis=1) - seg
    total = jnp.sum(seg, axis=0)
    padded = (total + te - 1) // te * te
    pad_end = jnp.cumsum(padded)
    pad_start = pad_end - padded
    gseg = pad_start[None, :] + jnp.cumsum(seg, axis=0) - seg
    n_used = (pad_end[-1] // te).astype(jnp.int32)
    blk = jnp.arange(n_blocks, dtype=jnp.int32)
    be = jnp.sum(pad_end[None, :] <= (blk * te)[:, None], axis=-1)
    block_expert = jnp.minimum(jnp.minimum(be, be[jnp.maximum(n_used - 1, 0)]),
                               N_EXPERTS - 1).astype(jnp.int32)
    fill = jnp.stack([jnp.concatenate([pad_start + total, pad_end[-1:]]),
                      jnp.concatenate([pad_end, jnp.full((1,), n_rows, pad_end.dtype)])]).astype(jnp.int32)
    flat = lambda a: jnp.pad(a, ((TABLE_LEAD, TABLE_TAIL), (0, 0))).reshape(-1).astype(jnp.int32)
    lss3 = jnp.pad(lss, ((0, 0), (0, LANES - N_EXPERTS))).reshape(n_tiles, 1, LANES).astype(jnp.int32)

    xs = _dispatch(flat(seg // SEG), flat(lss // SEG), flat(gseg // SEG), fill // SEG, xn2, top_idx,
                   rank, lss3, n_rows, tm, r_loc)
    ys = _moe(block_expert, n_used.reshape(1), xs.reshape(n_rows, D), moe_w1, moe_b1.reshape(N_EXPERTS, 1, -1),
              moe_w2, moe_b2.reshape(N_EXPERTS, 1, -1), te)
    out = _combine(flat(seg // SEG), flat(lss // SEG), flat(gseg // SEG), top_idx, rank, lss3, gates, x1,
                   row(final_scale), ys.reshape(n_rows // SEG, SEG, D), tm, r_loc)
    return out.reshape(B, S, D)


def kernel(x, norm1_scale, w_in, f_bias, gate_bias, conv_w, conv_b, rg_wa, rg_ba, rg_wx, rg_bx, rg_lambda, w_attn_proj, w_rec_proj, w_out, norm2_scale, router_w, router_b, moe_w1, moe_b1, moe_w2, moe_b2, final_norm_scale):
    depth = norm1_scale.shape[0]
    assert depth == 1, "the final rmsnorm is fused into the single layer's combine step"
    l = 0
    return _layer(x, norm1_scale[l], w_in[l], f_bias[l], gate_bias[l], conv_w[l], conv_b[l],
                  rg_wa[l], rg_ba[l], rg_wx[l], rg_bx[l], rg_lambda[l], w_attn_proj[l],
                  w_rec_proj[l], w_out[l], norm2_scale[l], router_w[l], router_b[l],
                  moe_w1[l], moe_b1[l], moe_w2[l], moe_b2[l], final_norm_scale)
```
